```python
import jax, jax.numpy as jnp
from jax import lax
import numpy as np

D_MODEL = 1024
BATCH = 16
SEQ = 2048
DEPTH = 2

CHUNK = 64
Q_BLOCK = 128
PLE_DIM = 256
D_FF = 2816
EPS = 1e-6
LB_FLOOR = 1e-30
MLA_HEADS = 8
MLA_NOPE = 64
MLA_ROPE = 32
MLA_V = 64
MLA_Q_LORA = 384
MLA_KV_LORA = 256
ROPE_BASE = 10000.0
SB_HEADS = 8
SB_HEAD_DIM = 64
HG_HEADS = 4
HG_KEY = 128
HG_VAL = 128
HG_SUB = 16
N_BRANCH = 3

MLA_WIDTH = MLA_HEADS * MLA_V
SB_WIDTH = SB_HEADS * SB_HEAD_DIM
HG_KEY_WIDTH = HG_HEADS * HG_KEY
HG_VAL_WIDTH = HG_HEADS * HG_VAL
IN_SPLITS = (MLA_Q_LORA, MLA_KV_LORA, MLA_ROPE,
             SB_WIDTH, SB_WIDTH, SB_WIDTH,
             HG_KEY_WIDTH, HG_KEY_WIDTH, HG_VAL_WIDTH, HG_VAL_WIDTH,
             N_BRANCH * D_MODEL)
IN_WIDTH = sum(IN_SPLITS)

kernel_name = "hybrid_mla_stickbreak_hgrn2_macaron"


def rms_norm(x, w):
    x32 = x.astype(jnp.float32)
    y = x32 * lax.rsqrt(jnp.mean(x32 * x32, axis=-1, keepdims=True) + EPS)
    return (y * w.astype(jnp.float32)).astype(x.dtype)


def split_cols(t, sizes):
    out, start = [], 0
    for n in sizes:
        out.append(t[..., start:start + n])
        start += n
    return out


def swiglu(x, w_in, w_out):
    g, up = jnp.split(x @ w_in, 2, axis=-1)
    return (jax.nn.silu(g) * up) @ w_out


def rope_tables(positions):
    half = MLA_ROPE // 2
    inv = ROPE_BASE ** (-jnp.arange(half, dtype=jnp.float32) / half)
    ang = positions.astype(jnp.float32)[..., None] * inv
    return jnp.cos(ang)[:, :, None, :], jnp.sin(ang)[:, :, None, :]


def apply_rope(x, cos, sin):
    half = MLA_ROPE // 2
    x32 = x.astype(jnp.float32)
    x1, x2 = x32[..., :half], x32[..., half:]
    return jnp.concatenate([x1 * cos - x2 * sin, x2 * cos + x1 * sin], axis=-1).astype(x.dtype)


def mla_attention(c_q, c_kv, k_rope, q_norm, w_uq, kv_norm, w_ukv, cos, sin):
    B, S, _ = c_q.shape
    q = (rms_norm(c_q, q_norm) @ w_uq).reshape(B, S, MLA_HEADS, MLA_NOPE + MLA_ROPE)
    q_nope, q_rope = q[..., :MLA_NOPE], apply_rope(q[..., MLA_NOPE:], cos, sin)
    kv = (rms_norm(c_kv, kv_norm) @ w_ukv).reshape(B, S, MLA_HEADS, MLA_NOPE + MLA_V)
    k_nope, v = kv[..., :MLA_NOPE], kv[..., MLA_NOPE:]
    k_r = apply_rope(k_rope[:, :, None, :], cos, sin)[:, :, 0, :]
    scale = (MLA_NOPE + MLA_ROPE) ** -0.5
    chunk_id = jnp.arange(S) // CHUNK
    outs = []
    for q0 in range(0, S, Q_BLOCK):
        q1 = q0 + Q_BLOCK
        s = (jnp.einsum('bqhd,bkhd->bhqk', q_nope[:, q0:q1], k_nope[:, :q1])
             + jnp.einsum('bqhr,bkr->bhqk', q_rope[:, q0:q1], k_r[:, :q1]))
        s = s.astype(jnp.float32) * scale
        mask = chunk_id[q0:q1, None] >= chunk_id[None, :q1]
        pr = jax.nn.softmax(jnp.where(mask, s, -jnp.inf), axis=-1).astype(v.dtype)
        outs.append(jnp.einsum('bhqk,bkhd->bqhd', pr, v[:, :q1]))
    return jnp.concatenate(outs, axis=1).reshape(B, S, MLA_WIDTH)


def stick_breaking_attention(q, k, v):
    B, S, _ = q.shape
    q = q.reshape(B, S, SB_HEADS, SB_HEAD_DIM)
    k = k.reshape(B, S, SB_HEADS, SB_HEAD_DIM)
    v = v.reshape(B, S, SB_HEADS, SB_HEAD_DIM)
    scale = SB_HEAD_DIM ** -0.5
    pos = jnp.arange(S)
    outs = []
    for q0 in range(0, S, Q_BLOCK):
        q1 = q0 + Q_BLOCK
        z = jnp.einsum('bqhd,bkhd->bhqk', q[:, q0:q1], k[:, :q1]).astype(jnp.float32) * scale
        mask = pos[None, :q1] < pos[q0:q1, None]
        log_beta = jax.nn.log_sigmoid(z)
        log_keep = jnp.where(mask, jax.nn.log_sigmoid(-z), 0.0)
        log_rest = lax.cumsum(log_keep, axis=3, reverse=True) - log_keep
        a = jnp.where(mask, jnp.exp(jnp.minimum(log_beta + log_rest, 0.0)), 0.0).astype(v.dtype)
        outs.append(jnp.einsum('bhqk,bkhd->bqhd', a, v[:, :q1]))
    return jnp.concatenate(outs, axis=1).reshape(B, S, SB_WIDTH)


def hgrn2_recurrence(q_raw, f_raw, i_in, g_raw, lb, norm_w):
    B, S, _ = q_raw.shape
    N = S // CHUNK
    NS = CHUNK // HG_SUB
    xf = f_raw.astype(jnp.float32)
    lb = jnp.clip(lb.astype(jnp.float32), 0.0, 1.0 - 1e-6)
    log_f = jnp.logaddexp(jnp.log(jnp.maximum(lb, LB_FLOOR)), jnp.log1p(-lb) + jax.nn.log_sigmoid(xf))
    log_f = jnp.minimum(log_f, 0.0)
    k = (1.0 - lb) * jax.nn.sigmoid(-xf)
    q = jax.nn.silu(q_raw.astype(jnp.float32))
    v = i_in.astype(jnp.float32)

    def to_chunks(t, d):
        return t.reshape(B, N, CHUNK, HG_HEADS, d).transpose(0, 3, 1, 2, 4)

    q, k, log_f = to_chunks(q, HG_KEY), to_chunks(k, HG_KEY), to_chunks(log_f, HG_KEY)
    v = to_chunks(v, HG_VAL)
    b = jnp.cumsum(log_f, axis=3)

    b_ref = jnp.concatenate([jnp.zeros_like(b[:, :, :, :1]),
                             b[:, :, :, HG_SUB - 1:CHUNK - 1:HG_SUB]], axis=3)
    b_sub = b.reshape(B, HG_HEADS, N, NS, HG_SUB, HG_KEY)
    qf = q.reshape(B, HG_HEADS, N, NS, HG_SUB, HG_KEY) * jnp.exp(b_sub - b_ref[:, :, :, :, None, :])
    s_idx = jnp.arange(CHUNK)
    valid = s_idx[None, :] < ((jnp.arange(NS) + 1) * HG_SUB)[:, None]
    expo = jnp.where(valid[:, :, None], b_ref[:, :, :, :, None, :] - b[:, :, :, None, :, :], -jnp.inf)
    kf = k[:, :, :, None] * jnp.exp(expo)
    a = jnp.einsum('bhnilk,bhnisk->bhnils', qf, kf)
    t_idx = jnp.arange(CHUNK).reshape(NS, HG_SUB)
    causal = s_idx[None, None, :] <= t_idx[:, :, None]
    a = jnp.where(causal, a, 0.0)
    o_intra = jnp.einsum('bhnils,bhnsv->bhnilv', a, v).reshape(B, HG_HEADS, N, CHUNK, HG_VAL)

    b_last = b[:, :, :, -1, :]
    qd = q * jnp.exp(b)
    kd = k * jnp.exp(b_last[:, :, :, None, :] - b)
    decay = jnp.exp(b_last)

    def step(state, xs):
        qd_n, kd_n, v_n, decay_n = xs
        o_n = jnp.einsum('bhck,bhkv->bhcv', qd_n, state)
        state = decay_n[..., None] * state + jnp.einsum('bhck,bhcv->bhkv', kd_n, v_n)
        return state, o_n

    xs = (jnp.moveaxis(qd, 2, 0), jnp.moveaxis(kd, 2, 0), jnp.moveaxis(v, 2, 0), jnp.moveaxis(decay, 2, 0))
    state0 = jnp.zeros((B, HG_HEADS, HG_KEY, HG_VAL), jnp.float32)
    _, o_inter = lax.scan(step, state0, xs)
    o = o_intra + jnp.moveaxis(o_inter, 0, 2)
    o = o.transpose(0, 2, 3, 1, 4).reshape(B, S, HG_HEADS, HG_VAL)
    o = o * lax.rsqrt(jnp.mean(o * o, axis=-1, keepdims=True) + EPS)
    o = o * norm_w.astype(jnp.float32).reshape(HG_HEADS, HG_VAL)
    o = o.reshape(B, S, HG_VAL_WIDTH) * jax.nn.silu(g_raw.astype(jnp.float32))
    return o.astype(q_raw.dtype)


def setup_inputs(seed: int = 0) -> dict:
    key = jax.random.key(seed)
    ks = jax.random.split(key, 32)
    f32 = jnp.float32

    def w(k, shape, fan_in):
        return jax.random.normal(k, shape, f32) * (fan_in ** -0.5)

    def gain(k, shape):
        return 1.0 + 0.05 * jax.random.normal(k, shape, f32)

    x = jax.random.normal(ks[0], (BATCH, SEQ, D_MODEL), f32)
    p = jax.random.normal(ks[1], (DEPTH, BATCH, SEQ, PLE_DIM), f32)
    offsets = jax.random.randint(ks[2], (BATCH, 1), 0, 64) * CHUNK
    positions = (offsets + jnp.arange(SEQ, dtype=jnp.int32)[None, :]).astype(jnp.int32)
    return {
        "x": x,
        "p": p,
        "positions": positions,
        "ffn_a_norm": gain(ks[3], (DEPTH, D_MODEL)),
        "ffn_a_w_in": w(ks[4], (DEPTH, D_MODEL, 2 * D_FF), D_MODEL),
        "ffn_a_w_out": w(ks[5], (DEPTH, D_FF, D_MODEL), D_FF),
        "mix_norm": gain(ks[6], (DEPTH, D_MODEL)),
        "w_in": w(ks[7], (DEPTH, D_MODEL, IN_WIDTH), D_MODEL),
        "mla_q_norm": gain(ks[8], (DEPTH, MLA_Q_LORA)),
        "mla_w_uq": w(ks[9], (DEPTH, MLA_Q_LORA, MLA_HEADS * (MLA_NOPE + MLA_ROPE)), MLA_Q_LORA),
        "mla_kv_norm": gain(ks[10], (DEPTH, MLA_KV_LORA)),
        "mla_w_ukv": w(ks[11], (DEPTH, MLA_KV_LORA, MLA_HEADS * (MLA_NOPE + MLA_V)), MLA_KV_LORA),
        "hgrn_lower_bounds": 0.5 * jax.random.normal(ks[12], (DEPTH, HG_KEY_WIDTH), f32),
        "hgrn_out_norm": gain(ks[13], (DEPTH, HG_VAL_WIDTH)),
        "w_br_mla": w(ks[14], (DEPTH, MLA_WIDTH, D_MODEL), MLA_WIDTH),
        "w_br_sb": w(ks[15], (DEPTH, SB_WIDTH, D_MODEL), SB_WIDTH),
        "w_br_hgrn": w(ks[16], (DEPTH, HG_VAL_WIDTH, D_MODEL), HG_VAL_WIDTH),
        "w_out": w(ks[17], (DEPTH, D_MODEL, D_MODEL), D_MODEL),
        "ffn_b_norm": gain(ks[18], (DEPTH, D_MODEL)),
        "ffn_b_w_in": w(ks[19], (DEPTH, D_MODEL, 2 * D_FF), D_MODEL),
        "ffn_b_w_out": w(ks[20], (DEPTH, D_FF, D_MODEL), D_FF),
        "ple_norm": gain(ks[21], (DEPTH, D_MODEL)),
        "w_ple_gate": w(ks[22], (DEPTH, D_MODEL, D_MODEL), D_MODEL),
        "w_ple_proj": w(ks[23], (DEPTH, PLE_DIM, D_MODEL), PLE_DIM),
        "final_norm": gain(ks[24], (D_MODEL,)),
    }


def reference(x, p, positions, ffn_a_norm, ffn_a_w_in, ffn_a_w_out, mix_norm, w_in,
              mla_q_norm, mla_w_uq, mla_kv_norm, mla_w_ukv, hgrn_lower_bounds, hgrn_out_norm,
              w_br_mla, w_br_sb, w_br_hgrn, w_out, ffn_b_norm, ffn_b_w_in, ffn_b_w_out,
              ple_norm, w_ple_gate, w_ple_proj, final_norm):
    B, S, D = x.shape
    cos, sin = rope_tables(positions)
    lb_sm = jax.nn.softmax(hgrn_lower_bounds.astype(jnp.float32), axis=0)
    lb_all = jnp.concatenate([jnp.zeros_like(lb_sm[:1]), jnp.cumsum(lb_sm[1:], axis=0)], axis=0)
    h = x
    for i in range(DEPTH):
        h = h + 0.5 * swiglu(rms_norm(h, ffn_a_norm[i]), ffn_a_w_in[i], ffn_a_w_out[i])
        u = rms_norm(h, mix_norm[i])
        (c_q, c_kv, k_rope, sb_q, sb_k, sb_v,
         hg_q, hg_f, hg_i, hg_g, gate_logits) = split_cols(u @ w_in[i], IN_SPLITS)
        y_a = mla_attention(c_q, c_kv, k_rope, mla_q_norm[i], mla_w_uq[i],
                            mla_kv_norm[i], mla_w_ukv[i], cos, sin)
        y_b = stick_breaking_attention(sb_q, sb_k, sb_v)
        y_c = hgrn2_recurrence(hg_q, hg_f, hg_i, hg_g, lb_all[i], hgrn_out_norm[i])
        gates = jax.nn.sigmoid(gate_logits.astype(jnp.float32)).astype(h.dtype).reshape(B, S, N_BRANCH, D)
        merged = (gates[:, :, 0] * (y_a @ w_br_mla[i])
                  + gates[:, :, 1] * (y_b @ w_br_sb[i])
                  + gates[:, :, 2] * (y_c @ w_br_hgrn[i]))
        h = h + merged @ w_out[i]
        h = h + 0.5 * swiglu(rms_norm(h, ffn_b_norm[i]), ffn_b_w_in[i], ffn_b_w_out[i])
        h = h + (p[i] @ w_ple_proj[i]) * jax.nn.sigmoid(rms_norm(h, ple_norm[i]) @ w_ple_gate[i])
    return rms_norm(h, final_norm)
```

```python
import functools

import jax
import jax.numpy as jnp
from jax import lax
from jax.experimental import pallas as pl
from jax.experimental.pallas import tpu as pltpu

F32 = jnp.float32
BF16 = jnp.bfloat16

EPS = 1e-6
LB_FLOOR = 1e-30
CHUNK = 64
HG_SUB = 16
MLA_HEADS = 8
MLA_NOPE = 64
MLA_ROPE = 32
MLA_V = 64
MLA_Q_LORA = 384
MLA_KV_LORA = 256
ROPE_BASE = 10000.0
SB_HEADS = 8
SB_HEAD_DIM = 64
HG_HEADS = 4
HG_KEY = 128
N_BRANCH = 3

LANES = 128
MLA_IN_WIDTH = MLA_Q_LORA + MLA_KV_LORA + 2 * LANES
ATT_BLOCK = 256
ROW_BLOCK = 512
VMEM_LIMIT_BYTES = 56 * 1024 * 1024

NT_DIMS = (((1,), (1,)), ((), ()))
TN_DIMS = (((0,), (0,)), ((), ()))


def _rms(x, w):
    return x * lax.rsqrt(jnp.mean(x * x, axis=-1, keepdims=True) + EPS) * w


def _dot(a, b):
    return jnp.dot(a, b, preferred_element_type=F32)


def _softplus(z):
    return jnp.maximum(z, 0.0) + jnp.log1p(jnp.exp(-jnp.abs(z)))


def _split_bf16(x, terms):
    parts = []
    for _ in range(terms - 1):
        hi = x.astype(BF16)
        parts.append(hi)
        x = x - hi.astype(F32)
    parts.append(x.astype(BF16))
    return parts


def _params(n_axes):
    return pltpu.CompilerParams(dimension_semantics=("arbitrary",) * n_axes,
                                vmem_limit_bytes=VMEM_LIMIT_BYTES)


def _resident(shape):
    return pl.BlockSpec(shape, lambda *_: (0,) * len(shape), pipeline_mode=pl.Buffered(1))


def _ffn_kernel(h_ref, nw_ref, win_ref, wout_ref, o_ref, xn_ref, hid_ref, *, d_ff, fc):
    x = h_ref[...]
    xn_ref[...] = _rms(x, nw_ref[...]).astype(BF16)
    for c in range(d_ff // fc):
        xn = xn_ref[...]
        g = _dot(xn, win_ref[:, c * fc:(c + 1) * fc])
        up = _dot(xn, win_ref[:, d_ff + c * fc:d_ff + (c + 1) * fc])
        hid_ref[:, c * fc:(c + 1) * fc] = (g * jax.nn.sigmoid(g) * up).astype(BF16)
    o_ref[...] = x + 0.5 * _dot(hid_ref[...], wout_ref[...])


def _ffn(h, nw, w_in, w_out):
    t, d = h.shape
    d_ff = w_out.shape[0]
    tm = min(ROW_BLOCK, t)
    return pl.pallas_call(
        functools.partial(_ffn_kernel, d_ff=d_ff, fc=256),
        grid=(t // tm,),
        in_specs=[pl.BlockSpec((tm, d), lambda i: (i, 0)),
                  _resident((1, d)), _resident((d, 2 * d_ff)), _resident((d_ff, d))],
        out_specs=pl.BlockSpec((tm, d), lambda i: (i, 0)),
        out_shape=jax.ShapeDtypeStruct((t, d), F32),
        scratch_shapes=[pltpu.VMEM((tm, d), BF16), pltpu.VMEM((tm, d_ff), BF16)],
        compiler_params=_params(1),
        name="ffn",
    )(h, nw, w_in, w_out)


def _inproj_kernel(h_ref, nw_ref, w_ref, *o_refs, widths, nc):
    u = _rms(h_ref[...], nw_ref[...]).astype(BF16)
    start = 0
    for o_ref, width in zip(o_refs, widths):
        for c in range(0, width, nc):
            n = min(nc, width - c)
            o_ref[:, c:c + n] = _dot(u, w_ref[:, start + c:start + c + n]).astype(o_ref.dtype)
        start += width


def _inproj(h, nw, w, widths, dtypes):
    t, d = h.shape
    tm = min(ROW_BLOCK, t)
    return pl.pallas_call(
        functools.partial(_inproj_kernel, widths=widths, nc=256),
        grid=(t // tm,),
        in_specs=[pl.BlockSpec((tm, d), lambda i: (i, 0)), _resident((1, d)), _resident(w.shape)],
        out_specs=[pl.BlockSpec((tm, n), lambda i: (i, 0)) for n in widths],
        out_shape=[jax.ShapeDtypeStruct((t, n), dt) for n, dt in zip(widths, dtypes)],
        compiler_params=_params(1),
        name="inproj",
    )(h, nw, w)


def _mla_kernel(x_ref, pos_ref, inv_ref, qn_ref, kvn_ref, wqn_ref, wqr_ref, wqrp_ref, wk_ref, wv_ref,
                o_ref, qcat_ref, kcat_ref, v_ref, *, seq, tq, scale):
    p = pl.program_id(1)
    n_pairs = MLA_HEADS // 2

    @pl.when(p == 0)
    def _project():
        rb = min(ROW_BLOCK, seq)
        for r in range(seq // rb):
            rows = slice(r * rb, (r + 1) * rb)
            x = x_ref[0, rows, :]
            c_q = x[:, :MLA_Q_LORA].astype(F32)
            c_kv = x[:, MLA_Q_LORA:MLA_Q_LORA + MLA_KV_LORA].astype(F32)
            kr = x[:, MLA_Q_LORA + MLA_KV_LORA:MLA_Q_LORA + MLA_KV_LORA + LANES].astype(F32)
            krp = x[:, MLA_Q_LORA + MLA_KV_LORA + LANES:].astype(F32)
            ang = pos_ref[0, rows, :].astype(F32) * inv_ref[...]
            cos, sin = jnp.cos(ang), jnp.sin(ang)
            cqn = _rms(c_q, qn_ref[...]).astype(BF16)
            ckvn = _rms(c_kv, kvn_ref[...]).astype(BF16)
            k_rope = (kr * cos + krp * sin).astype(BF16)
            for g in range(2):
                cols = slice(g * LANES, (g + 1) * LANES)
                q_rope = ((_dot(cqn, wqr_ref[:, cols]) * cos + _dot(cqn, wqrp_ref[:, cols]) * sin)
                          * scale).astype(BF16)
                qcat_ref[2 * g, rows, LANES:] = q_rope
                qcat_ref[2 * g + 1, rows, LANES:] = q_rope
            for pp in range(n_pairs):
                cols = slice(pp * LANES, (pp + 1) * LANES)
                qcat_ref[pp, rows, :LANES] = (_dot(cqn, wqn_ref[:, cols]) * scale).astype(BF16)
                kcat_ref[pp, rows, :LANES] = _dot(ckvn, wk_ref[:, cols]).astype(BF16)
                kcat_ref[pp, rows, LANES:] = k_rope
                v_ref[pp, rows, :] = _dot(ckvn, wv_ref[:, cols]).astype(BF16)

    lane = lax.broadcasted_iota(jnp.int32, (1, 2 * LANES), 1)
    out_lane = lax.broadcasted_iota(jnp.int32, (1, LANES), 1)
    ri = lax.broadcasted_iota(jnp.int32, (tq, tq), 0)
    ci = lax.broadcasted_iota(jnp.int32, (tq, tq), 1)
    diag_mask = (ci // CHUNK) <= (ri // CHUNK)

    def q_block(qi, carry):
        q0 = pl.multiple_of(qi * tq, tq)
        qc = qcat_ref[p, pl.ds(q0, tq), :]
        outs = []
        for hh in range(2):
            group = jnp.where(lane < LANES, lane // MLA_NOPE, 2 + (lane - LANES) // MLA_ROPE)
            own = (group == hh) | (group == 2 + (p % 2) * 2 + hh)
            qh = qc * jnp.where(own, 1.0, 0.0).astype(BF16)

            def scores(k0):
                return lax.dot_general(qh, kcat_ref[p, pl.ds(k0, tq), :], NT_DIMS,
                                       preferred_element_type=F32)

            s = jnp.where(diag_mask, scores(q0), -jnp.inf)
            m = jnp.max(s, axis=-1, keepdims=True)
            pr = jnp.exp(s - m)
            l = jnp.sum(pr, axis=-1, keepdims=True)
            acc = _dot(pr.astype(BF16), v_ref[p, pl.ds(q0, tq), :])

            def k_block(kb, state):
                m, l, acc = state
                k0 = pl.multiple_of(kb * tq, tq)
                s = scores(k0)
                m_new = jnp.maximum(m, jnp.max(s, axis=-1, keepdims=True))
                alpha = jnp.exp(m - m_new)
                pr = jnp.exp(s - m_new)
                l = alpha * l + jnp.sum(pr, axis=-1, keepdims=True)
                acc = alpha * acc + _dot(pr.astype(BF16), v_ref[p, pl.ds(k0, tq), :])
                return m_new, l, acc

            m, l, acc = lax.fori_loop(0, qi, k_block, (m, l, acc))
            outs.append(acc / l)
        o_ref[0, pl.ds(q0, tq), :] = jnp.where(out_lane < MLA_V, outs[0], outs[1]).astype(o_ref.dtype)
        return carry

    lax.fori_loop(0, seq // tq, q_block, 0)


def _mla(x, pos, inv, qn, kvn, wqn, wqr, wqrp, wk, wv):
    b, seq, _ = x.shape
    n_pairs = MLA_HEADS // 2
    tq = min(ATT_BLOCK, seq)
    scale = float((MLA_NOPE + MLA_ROPE) ** -0.5)
    return pl.pallas_call(
        functools.partial(_mla_kernel, seq=seq, tq=tq, scale=scale),
        grid=(b, n_pairs),
        in_specs=[pl.BlockSpec((1, seq, MLA_IN_WIDTH), lambda i, j: (i, 0, 0)),
                  pl.BlockSpec((1, seq, 1), lambda i, j: (i, 0, 0)),
                  _resident(inv.shape), _resident(qn.shape), _resident(kvn.shape),
                  _resident(wqn.shape), _resident(wqr.shape), _resident(wqrp.shape),
                  _resident(wk.shape), _resident(wv.shape)],
        out_specs=pl.BlockSpec((1, seq, LANES), lambda i, j: (i, 0, j)),
        out_shape=jax.ShapeDtypeStruct((b, seq, MLA_HEADS * MLA_V), BF16),
        scratch_shapes=[pltpu.VMEM((n_pairs, seq, 2 * LANES), BF16),
                        pltpu.VMEM((n_pairs, seq, 2 * LANES), BF16),
                        pltpu.VMEM((n_pairs, seq, LANES), BF16)],
        compiler_params=_params(2),
        name="mla",
    )(x, pos, inv, qn, kvn, wqn, wqr, wqrp, wk, wv)


def _sb_kernel(q_ref, k_ref, v_ref, tri_ref, o_ref, *, tq):
    qi = pl.program_id(2)
    q0 = pl.multiple_of(qi * tq, tq)
    q = q_ref[0]
    lane = lax.broadcasted_iota(jnp.int32, (1, LANES), 1)
    ri = lax.broadcasted_iota(jnp.int32, (tq, tq), 0)
    ci = lax.broadcasted_iota(jnp.int32, (tq, tq), 1)
    diag_mask = ci < ri
    scale = SB_HEAD_DIM ** -0.5

    outs = []
    for hh in range(2):
        qh = jnp.where((lane // SB_HEAD_DIM) == hh, q, jnp.zeros_like(q)) * jnp.asarray(scale, q.dtype)

        def block(k0, rest, acc, masked):
            z = lax.dot_general(qh, k_ref[0, pl.ds(k0, tq), :], NT_DIMS, preferred_element_type=F32)
            sp = _softplus(z)
            keep = jnp.where(diag_mask, sp, 0.0) if masked else sp
            hi, lo = _split_bf16(keep, 2)
            r = _dot(hi, tri_ref[...]) + _dot(lo, tri_ref[...])
            later = r[:, :tq]
            arg = (z - sp) - later - jnp.concatenate([rest] * (tq // LANES), axis=1)
            a = jnp.exp(jnp.minimum(arg, 0.0))
            if masked:
                a = jnp.where(diag_mask, a, 0.0)
            acc = acc + _dot(a.astype(BF16), v_ref[0, pl.ds(k0, tq), :])
            return rest + r[:, tq:], acc

        zeros = jnp.zeros((tq, LANES), F32)
        rest, acc = block(q0, zeros, zeros, True)

        def k_block(j, state):
            k0 = pl.multiple_of((qi - 1 - j) * tq, tq)
            return block(k0, state[0], state[1], False)

        rest, acc = lax.fori_loop(0, qi, k_block, (rest, acc))
        outs.append(acc)
    o_ref[0] = jnp.where(lane < SB_HEAD_DIM, outs[0], outs[1]).astype(o_ref.dtype)


def _sb(qkv, tri):
    b, seq, _ = qkv.shape
    n_pairs = SB_HEADS // 2
    tq = min(ATT_BLOCK, seq)
    return pl.pallas_call(
        functools.partial(_sb_kernel, tq=tq),
        grid=(b, n_pairs, seq // tq),
        in_specs=[pl.BlockSpec((1, tq, LANES), lambda i, j, k: (i, k, j)),
                  pl.BlockSpec((1, seq, LANES), lambda i, j, k: (i, 0, n_pairs + j)),
                  pl.BlockSpec((1, seq, LANES), lambda i, j, k: (i, 0, 2 * n_pairs + j)),
                  _resident(tri.shape)],
        out_specs=pl.BlockSpec((1, tq, LANES), lambda i, j, k: (i, k, j)),
        out_shape=jax.ShapeDtypeStruct((b, seq, SB_HEADS * SB_HEAD_DIM), BF16),
        compiler_params=_params(3),
        name="stickbreak",
    )(qkv, qkv, qkv, tri)


def _hgrn_kernel(q_ref, f_ref, i_ref, g_ref, lbp_ref, nw_ref, tri_ref, o_ref, b_ref, *, seq, layer):
    lbp = lbp_ref[...]
    e = jnp.exp(lbp - jnp.max(lbp, axis=0, keepdims=True))
    sm = e / jnp.sum(e, axis=0, keepdims=True)
    lb = jnp.zeros((1, HG_KEY), F32)
    for j in range(1, layer + 1):
        lb = lb + sm[j:j + 1, :]
    lb = jnp.clip(lb, 0.0, 1.0 - 1e-6)
    log_lb = jnp.log(jnp.maximum(lb, LB_FLOOR))
    log1m_lb = jnp.log1p(-lb)

    gb = tri_ref.shape[0]
    for r in range(seq // gb):
        rows = slice(r * gb, (r + 1) * gb)
        c = log1m_lb - _softplus(-f_ref[0, rows, :])
        log_f = jnp.minimum(jnp.maximum(log_lb, c) + jnp.log1p(jnp.exp(-jnp.abs(log_lb - c))), 0.0)
        acc = jnp.zeros((gb, HG_KEY), F32)
        for part in _split_bf16(log_f, 3):
            acc = acc + _dot(tri_ref[...], part)
        b_ref[rows, :] = acc

    n_sub = CHUNK // HG_SUB

    def chunk(n, state_t):
        r0 = pl.multiple_of(n * CHUNK, CHUNK)
        rows = pl.ds(r0, CHUNK)
        x = f_ref[0, rows, :]
        b = b_ref[rows, :]
        q_raw = q_ref[0, rows, :]
        q = q_raw * jax.nn.sigmoid(q_raw)
        k = (1.0 - lb) * jax.nn.sigmoid(-x)
        v = i_ref[0, rows, :]
        v16 = v.astype(BF16)
        b_last = b[CHUNK - 1:CHUNK, :]

        qd = (q * jnp.exp(b)).astype(BF16)
        kd = (k * jnp.exp(b_last - b)).astype(BF16)
        o = lax.dot_general(qd, state_t.astype(BF16), NT_DIMS, preferred_element_type=F32)
        update = lax.dot_general(v16, kd, TN_DIMS, preferred_element_type=F32)
        state_t = state_t * jnp.exp(b_last) + update

        intra = []
        for i in range(n_sub):
            lo, hi = i * HG_SUB, (i + 1) * HG_SUB
            b_ref_i = jnp.zeros((1, HG_KEY), F32) if i == 0 else b[lo - 1:lo, :]
            qf = (q[lo:hi, :] * jnp.exp(b[lo:hi, :] - b_ref_i)).astype(BF16)
            kf = (k[:hi, :] * jnp.exp(b_ref_i - b[:hi, :])).astype(BF16)
            a = lax.dot_general(qf, kf, NT_DIMS, preferred_element_type=F32)
            causal = (lax.broadcasted_iota(jnp.int32, (HG_SUB, hi), 1)
                      <= lax.broadcasted_iota(jnp.int32, (HG_SUB, hi), 0) + lo)
            a = jnp.where(causal, a, 0.0)
            intra.append(_dot(a.astype(BF16), v16[:hi, :]))
        o = o + jnp.concatenate(intra, axis=0)

        o = o * lax.rsqrt(jnp.mean(o * o, axis=-1, keepdims=True) + EPS) * nw_ref[...]
        g = g_ref[0, rows, :]
        o_ref[0, rows, :] = (o * (g * jax.nn.sigmoid(g))).astype(o_ref.dtype)
        return state_t

    lax.fori_loop(0, seq // CHUNK, chunk, jnp.zeros((HG_KEY, HG_KEY), F32))


def _hgrn(hg, lbp, nw, tri, layer):
    b, seq, _ = hg.shape
    depth = lbp.shape[0]
    col = lambda off: (lambda i, j: (i, 0, off + j))
    return pl.pallas_call(
        functools.partial(_hgrn_kernel, seq=seq, layer=layer),
        grid=(b, HG_HEADS),
        in_specs=[pl.BlockSpec((1, seq, HG_KEY), col(0)),
                  pl.BlockSpec((1, seq, HG_KEY), col(HG_HEADS)),
                  pl.BlockSpec((1, seq, HG_KEY), col(2 * HG_HEADS)),
                  pl.BlockSpec((1, seq, HG_KEY), col(3 * HG_HEADS)),
                  pl.BlockSpec((depth, HG_KEY), lambda i, j: (0, j)),
                  pl.BlockSpec((1, HG_KEY), lambda i, j: (0, j)),
                  _resident(tri.shape)],
        out_specs=pl.BlockSpec((1, seq, HG_KEY), lambda i, j: (i, 0, j)),
        out_shape=jax.ShapeDtypeStruct((b, seq, HG_HEADS * HG_KEY), BF16),
        scratch_shapes=[pltpu.VMEM((seq, HG_KEY), F32)],
        compiler_params=_params(2),
        name="hgrn2",
    )(hg, hg, hg, hg, lbp, nw, tri)


def _merge_kernel(h_ref, ya_ref, yb_ref, yc_ref, gate_ref, wa_ref, wb_ref, wc_ref, wo_ref, o_ref):
    d = h_ref.shape[1]
    merged = None
    for j, (y_ref, w_ref) in enumerate(((ya_ref, wa_ref), (yb_ref, wb_ref), (yc_ref, wc_ref))):
        gate = jax.nn.sigmoid(gate_ref[:, j * d:(j + 1) * d].astype(F32))
        term = gate * _dot(y_ref[...], w_ref[...])
        merged = term if merged is None else merged + term
    o_ref[...] = h_ref[...] + _dot(merged.astype(BF16), wo_ref[...])


def _merge(h, ya, yb, yc, gates, wa, wb, wc, wo):
    t, d = h.shape
    tm = min(ROW_BLOCK, t)
    row = lambda n: pl.BlockSpec((tm, n), lambda i: (i, 0))
    return pl.pallas_call(
        _merge_kernel,
        grid=(t // tm,),
        in_specs=[row(d), row(ya.shape[1]), row(yb.shape[1]), row(yc.shape[1]), row(gates.shape[1]),
                  _resident(wa.shape), _resident(wb.shape), _resident(wc.shape), _resident(wo.shape)],
        out_specs=row(d),
        out_shape=jax.ShapeDtypeStruct((t, d), F32),
        compiler_params=_params(1),
        name="merge",
    )(h, ya, yb, yc, gates, wa, wb, wc, wo)


def _ple_kernel(h_ref, p_ref, nw_ref, wg_ref, wp_ref, fw_ref, o_ref, *, final):
    h = h_ref[...]
    gate = jax.nn.sigmoid(_dot(_rms(h, nw_ref[...]).astype(BF16), wg_ref[...]))
    h = h + _dot(p_ref[...].astype(BF16), wp_ref[...]) * gate
    o_ref[...] = _rms(h, fw_ref[...]) if final else h


def _ple(h, p, nw, wg, wp, fw, final):
    t, d = h.shape
    tm = min(ROW_BLOCK, t)
    row = lambda n: pl.BlockSpec((tm, n), lambda i: (i, 0))
    return pl.pallas_call(
        functools.partial(_ple_kernel, final=final),
        grid=(t // tm,),
        in_specs=[row(d), row(p.shape[1]), _resident((1, d)), _resident(wg.shape), _resident(wp.shape),
                  _resident((1, d))],
        out_specs=row(d),
        out_shape=jax.ShapeDtypeStruct((t, d), F32),
        compiler_params=_params(1),
        name="ple",
    )(h, p, nw, wg, wp, fw)


def _rot_half(w):
    half = MLA_ROPE // 2
    return jnp.concatenate([-w[..., half:], w[..., :half]], axis=-1)


def _strict_lower(n):
    r = lax.broadcasted_iota(jnp.int32, (n, n), 0)
    c = lax.broadcasted_iota(jnp.int32, (n, n), 1)
    return r > c


def kernel(x, p, positions, ffn_a_norm, ffn_a_w_in, ffn_a_w_out, mix_norm, w_in, mla_q_norm, mla_w_uq, mla_kv_norm, mla_w_ukv, hgrn_lower_bounds, hgrn_out_norm, w_br_mla, w_br_sb, w_br_hgrn, w_out, ffn_b_norm, ffn_b_w_in, ffn_b_w_out, ple_norm, w_ple_gate, w_ple_proj, final_norm):
    b, seq, d = x.shape
    depth = ffn_a_norm.shape[0]
    t = b * seq
    bf = lambda a: a.astype(BF16)
    row = lambda a: a.reshape(1, -1).astype(F32)

    tq = min(ATT_BLOCK, seq)
    sb_tri = jnp.concatenate([_strict_lower(tq), jnp.ones((tq, LANES), bool)], axis=1).astype(BF16)
    gb = 4 * CHUNK
    r = lax.broadcasted_iota(jnp.int32, (gb, gb), 0)
    c = lax.broadcasted_iota(jnp.int32, (gb, gb), 1)
    hg_tri = ((c <= r) & (c // CHUNK == r // CHUNK)).astype(BF16)

    half = MLA_ROPE // 2
    inv = ROPE_BASE ** (-jnp.arange(half, dtype=F32) / half)
    inv = jnp.tile(inv, LANES // half).reshape(1, LANES)
    pos = positions.reshape(b, seq, 1)

    splits = (MLA_Q_LORA, MLA_KV_LORA, MLA_ROPE, 3 * SB_HEADS * SB_HEAD_DIM,
              4 * HG_HEADS * HG_KEY, N_BRANCH * d)
    widths = (MLA_IN_WIDTH, splits[3], splits[4], splits[5])

    h = x.reshape(t, d)
    for i in range(depth):
        h = _ffn(h, row(ffn_a_norm[i]), bf(ffn_a_w_in[i]), bf(ffn_a_w_out[i]))

        w = w_in[i]
        offs = [0]
        for n in splits:
            offs.append(offs[-1] + n)
        w_cq, w_ckv, w_kr, w_sb, w_hg, w_gate = (w[:, offs[j]:offs[j + 1]] for j in range(6))
        w_cat = bf(jnp.concatenate([w_cq, w_ckv, jnp.tile(w_kr, (1, 4)), jnp.tile(_rot_half(w_kr), (1, 4)),
                                    w_sb, w_hg, w_gate], axis=1))
        mla_in, sb_in, hg_in, gates = _inproj(h, row(mix_norm[i]), w_cat, widths, (BF16, BF16, F32, BF16))

        wq = mla_w_uq[i].reshape(MLA_Q_LORA, MLA_HEADS, MLA_NOPE + MLA_ROPE)
        wq_nope = bf(wq[:, :, :MLA_NOPE].reshape(MLA_Q_LORA, -1))
        wq_rope = wq[:, :, MLA_NOPE:]
        wkv = mla_w_ukv[i].reshape(MLA_KV_LORA, MLA_HEADS, MLA_NOPE + MLA_V)
        y_a = _mla(mla_in.reshape(b, seq, -1), pos, inv, row(mla_q_norm[i]), row(mla_kv_norm[i]),
                   wq_nope, bf(wq_rope.reshape(MLA_Q_LORA, -1)), bf(_rot_half(wq_rope).reshape(MLA_Q_LORA, -1)),
                   bf(wkv[:, :, :MLA_NOPE].reshape(MLA_KV_LORA, -1)), bf(wkv[:, :, MLA_NOPE:].reshape(MLA_KV_LORA, -1)))
        y_b = _sb(sb_in.reshape(b, seq, -1), sb_tri)
        y_c = _hgrn(hg_in.reshape(b, seq, -1), hgrn_lower_bounds.astype(F32), row(hgrn_out_norm[i]), hg_tri, i)

        h = _merge(h, y_a.reshape(t, -1), y_b.reshape(t, -1), y_c.reshape(t, -1), gates,
                   bf(w_br_mla[i]), bf(w_br_sb[i]), bf(w_br_hgrn[i]), bf(w_out[i]))
        h = _ffn(h, row(ffn_b_norm[i]), bf(ffn_b_w_in[i]), bf(ffn_b_w_out[i]))
        h = _ple(h, p[i].reshape(t, -1), row(ple_norm[i]), bf(w_ple_gate[i]), bf(w_ple_proj[i]),
                 row(final_norm), final=(i == depth - 1))
    return h.reshape(b, seq, d)
```

```python
import functools

import jax
import jax.numpy as jnp
from jax import lax
from jax.experimental import pallas as pl
from jax.experimental.pallas import tpu as pltpu

F32 = jnp.float32
BF16 = jnp.bfloat16

EPS = 1e-6
LB_FLOOR = 1e-30
CHUNK = 64
HG_SUB = 16
MLA_HEADS = 8
MLA_NOPE = 64
MLA_ROPE = 32
MLA_V = 64
MLA_Q_LORA = 384
MLA_KV_LORA = 256
ROPE_BASE = 10000.0
LOG2E = 1.4426950408889634
INV_LN2 = LOG2E
SB_HEADS = 8
SB_HEAD_DIM = 64
HG_HEADS = 4
HG_KEY = 128
N_BRANCH = 3

LANES = 128
MLA_IN_WIDTH = MLA_Q_LORA + MLA_KV_LORA + 2 * LANES
ATT_BLOCK = 256
ROW_BLOCK = 512
VMEM_LIMIT_BYTES = 56 * 1024 * 1024

NT_DIMS = (((1,), (1,)), ((), ()))
TN_DIMS = (((0,), (0,)), ((), ()))


def _rms(x, w):
    return x * lax.rsqrt(jnp.mean(x * x, axis=-1, keepdims=True) + EPS) * w


def _dot(a, b):
    return jnp.dot(a, b, preferred_element_type=F32)


def _softplus(z):
    return jnp.maximum(z, 0.0) + jnp.log1p(jnp.exp(-jnp.abs(z)))


def _split_bf16(x, terms):
    parts = []
    for _ in range(terms - 1):
        hi = x.astype(BF16)
        parts.append(hi)
        x = x - hi.astype(F32)
    parts.append(x.astype(BF16))
    return parts


def _params(n_axes):
    return pltpu.CompilerParams(dimension_semantics=("arbitrary",) * n_axes,
                                vmem_limit_bytes=VMEM_LIMIT_BYTES)


def _resident(shape):
    return pl.BlockSpec(shape, lambda *_: (0,) * len(shape), pipeline_mode=pl.Buffered(1))


def _ffn_kernel(h_ref, nw_ref, win_ref, wout_ref, o_ref, xn_ref, hid_ref, *, d_ff, fc):
    x = h_ref[...]
    xn_ref[...] = _rms(x, nw_ref[...]).astype(BF16)
    for c in range(d_ff // fc):
        xn = xn_ref[...]
        g = _dot(xn, win_ref[:, c * fc:(c + 1) * fc])
        up = _dot(xn, win_ref[:, d_ff + c * fc:d_ff + (c + 1) * fc])
        hid_ref[:, c * fc:(c + 1) * fc] = (g * jax.nn.sigmoid(g) * up).astype(BF16)
    o_ref[...] = x + 0.5 * _dot(hid_ref[...], wout_ref[...])


def _ffn(h, nw, w_in, w_out):
    t, d = h.shape
    d_ff = w_out.shape[0]
    tm = min(ROW_BLOCK, t)
    return pl.pallas_call(
        functools.partial(_ffn_kernel, d_ff=d_ff, fc=256),
        grid=(t // tm,),
        in_specs=[pl.BlockSpec((tm, d), lambda i: (i, 0)),
                  _resident((1, d)), _resident((d, 2 * d_ff)), _resident((d_ff, d))],
        out_specs=pl.BlockSpec((tm, d), lambda i: (i, 0)),
        out_shape=jax.ShapeDtypeStruct((t, d), F32),
        scratch_shapes=[pltpu.VMEM((tm, d), BF16), pltpu.VMEM((tm, d_ff), BF16)],
        compiler_params=_params(1),
        name="ffn",
    )(h, nw, w_in, w_out)


def _inproj_kernel(h_ref, nw_ref, w_ref, *o_refs, widths, nc):
    u = _rms(h_ref[...], nw_ref[...]).astype(BF16)
    start = 0
    for o_ref, width in zip(o_refs, widths):
        for c in range(0, width, nc):
            n = min(nc, width - c)
            o_ref[:, c:c + n] = _dot(u, w_ref[:, start + c:start + c + n]).astype(o_ref.dtype)
        start += width


def _inproj(h, nw, w, widths, dtypes):
    t, d = h.shape
    tm = min(ROW_BLOCK, t)
    return pl.pallas_call(
        functools.partial(_inproj_kernel, widths=widths, nc=256),
        grid=(t // tm,),
        in_specs=[pl.BlockSpec((tm, d), lambda i: (i, 0)), _resident((1, d)), _resident(w.shape)],
        out_specs=[pl.BlockSpec((tm, n), lambda i: (i, 0)) for n in widths],
        out_shape=[jax.ShapeDtypeStruct((t, n), dt) for n, dt in zip(widths, dtypes)],
        compiler_params=_params(1),
        name="inproj",
    )(h, nw, w)


def _rope_kernel(pos_ref, inv_ref, cos_ref, sin_ref):
    ang = pos_ref[0].astype(F32) * inv_ref[...]
    cos_ref[0] = jnp.cos(ang)
    sin_ref[0] = jnp.sin(ang)


def _rope_tables(pos, inv):
    b, seq, _ = pos.shape
    out = jax.ShapeDtypeStruct((b, seq, LANES), F32)
    spec = pl.BlockSpec((1, seq, LANES), lambda i: (i, 0, 0))
    return pl.pallas_call(
        _rope_kernel,
        grid=(b,),
        in_specs=[pl.BlockSpec((1, seq, 1), lambda i: (i, 0, 0)), _resident(inv.shape)],
        out_specs=[spec, spec],
        out_shape=[out, out],
        compiler_params=_params(1),
        name="rope_tables",
    )(pos, inv)


def _mla_kernel(x_ref, cos_ref, sin_ref, qn_ref, kvn_ref, wqn_ref, wqr_ref, wqrp_ref, wk_ref, wv_ref,
                o_ref, qcat_ref, kcat_ref, v_ref, *, seq, tq, scale):
    p = pl.program_id(1)
    qi = pl.program_id(2)
    n_pairs = MLA_HEADS // 2

    @pl.when((p == 0) & (qi == 0))
    def _project():
        rb = min(ROW_BLOCK, seq)
        for r in range(seq // rb):
            rows = slice(r * rb, (r + 1) * rb)
            x = x_ref[0, rows, :]
            c_q = x[:, :MLA_Q_LORA].astype(F32)
            c_kv = x[:, MLA_Q_LORA:MLA_Q_LORA + MLA_KV_LORA].astype(F32)
            kr = x[:, MLA_Q_LORA + MLA_KV_LORA:MLA_Q_LORA + MLA_KV_LORA + LANES].astype(F32)
            krp = x[:, MLA_Q_LORA + MLA_KV_LORA + LANES:].astype(F32)
            cos, sin = cos_ref[0, rows, :], sin_ref[0, rows, :]
            cqn = _rms(c_q, qn_ref[...]).astype(BF16)
            ckvn = _rms(c_kv, kvn_ref[...]).astype(BF16)
            k_rope = (kr * cos + krp * sin).astype(BF16)
            for g in range(2):
                cols = slice(g * LANES, (g + 1) * LANES)
                q_rope = ((_dot(cqn, wqr_ref[:, cols]) * cos + _dot(cqn, wqrp_ref[:, cols]) * sin)
                          * scale).astype(BF16)
                qcat_ref[2 * g, rows, LANES:] = q_rope
                qcat_ref[2 * g + 1, rows, LANES:] = q_rope
            for pp in range(n_pairs):
                cols = slice(pp * LANES, (pp + 1) * LANES)
                qcat_ref[pp, rows, :LANES] = (_dot(cqn, wqn_ref[:, cols]) * scale).astype(BF16)
                kcat_ref[pp, rows, :LANES] = _dot(ckvn, wk_ref[:, cols]).astype(BF16)
                kcat_ref[pp, rows, LANES:] = k_rope
                v_ref[pp, rows, :] = _dot(ckvn, wv_ref[:, cols]).astype(BF16)

    lane = lax.broadcasted_iota(jnp.int32, (1, 2 * LANES), 1)
    out_lane = lax.broadcasted_iota(jnp.int32, (1, LANES), 1)
    ri = lax.broadcasted_iota(jnp.int32, (tq, tq), 0)
    ci = lax.broadcasted_iota(jnp.int32, (tq, tq), 1)
    diag_mask = (ci // CHUNK) <= (ri // CHUNK)
    group = jnp.where(lane < LANES, lane // MLA_NOPE, 2 + (lane - LANES) // MLA_ROPE)

    def whole_row(n):
        def run():
            rows = slice(n * tq, (n + 1) * tq)
            qc = qcat_ref[p, rows, :]
            keys = [slice(j * tq, (j + 1) * tq) for j in range(n + 1)]
            s = []
            for hh in range(2):
                own = (group == hh) | (group == 2 + (p % 2) * 2 + hh)
                qh = qc * jnp.where(own, 1.0, 0.0).astype(BF16)
                sh = [lax.dot_general(qh, kcat_ref[p, kj, :], NT_DIMS, preferred_element_type=F32)
                      for kj in keys]
                sh[n] = jnp.where(diag_mask, sh[n], -jnp.inf)
                s.append(sh)
            m = [jnp.max(functools.reduce(jnp.maximum, sh), axis=-1, keepdims=True) for sh in s]
            outs = []
            for hh in range(2):
                pr = [jnp.exp2(sj - m[hh]) for sj in s[hh]]
                l = jnp.sum(functools.reduce(jnp.add, pr), axis=-1, keepdims=True)
                acc = None
                for j, kj in enumerate(keys):
                    t = _dot(pr[j].astype(BF16), v_ref[p, kj, :])
                    acc = t if acc is None else acc + t
                outs.append(acc * (1.0 / l))
            o_ref[0] = jnp.where(out_lane < MLA_V, outs[0], outs[1]).astype(o_ref.dtype)
        return run

    for n in range(seq // tq):
        pl.when(qi == n)(whole_row(n))


def _mla(x, cos, sin, qn, kvn, wqn, wqr, wqrp, wk, wv):
    b, seq, _ = x.shape
    n_pairs = MLA_HEADS // 2
    tq = min(ATT_BLOCK, seq)
    scale = float((MLA_NOPE + MLA_ROPE) ** -0.5) * LOG2E
    return pl.pallas_call(
        functools.partial(_mla_kernel, seq=seq, tq=tq, scale=scale),
        grid=(b, n_pairs, seq // tq),
        in_specs=[pl.BlockSpec((1, seq, MLA_IN_WIDTH), lambda i, j, k: (i, 0, 0)),
                  pl.BlockSpec((1, seq, LANES), lambda i, j, k: (i, 0, 0)),
                  pl.BlockSpec((1, seq, LANES), lambda i, j, k: (i, 0, 0)),
                  _resident(qn.shape), _resident(kvn.shape),
                  _resident(wqn.shape), _resident(wqr.shape), _resident(wqrp.shape),
                  _resident(wk.shape), _resident(wv.shape)],
        out_specs=pl.BlockSpec((1, tq, LANES), lambda i, j, k: (i, k, j)),
        out_shape=jax.ShapeDtypeStruct((b, seq, MLA_HEADS * MLA_V), BF16),
        scratch_shapes=[pltpu.VMEM((n_pairs, seq, 2 * LANES), BF16),
                        pltpu.VMEM((n_pairs, seq, 2 * LANES), BF16),
                        pltpu.VMEM((n_pairs, seq, LANES), BF16)],
        compiler_params=_params(3),
        name="mla",
    )(x, cos, sin, qn, kvn, wqn, wqr, wqrp, wk, wv)


def _sb_kernel(q_ref, k_ref, v_ref, tri_ref, o_ref, *, tq, n_q):
    qi = pl.program_id(2)
    q = q_ref[0]
    lane = lax.broadcasted_iota(jnp.int32, (1, LANES), 1)
    ri = lax.broadcasted_iota(jnp.int32, (tq, tq), 0)
    ci = lax.broadcasted_iota(jnp.int32, (tq, tq), 1)
    diag_mask = ci < ri
    qhs = [jnp.where((lane // SB_HEAD_DIM) == hh, q, jnp.zeros_like(q)) for hh in range(2)]
    sign = jnp.int32(-2 ** 31)

    zeros = jnp.zeros((tq, LANES), F32)

    def whole_row(n):
        def run():
            tiles = [(hh, j) for j in range(n, -1, -1) for hh in range(2)]
            keys = {j: slice(j * tq, (j + 1) * tq) for j in range(n + 1)}
            z2 = {t: lax.dot_general(qhs[t[0]], k_ref[0, keys[t[1]], :], NT_DIMS,
                                     preferred_element_type=F32) for t in tiles}
            log_beta, r = {}, {}
            for t in tiles:
                neg_abs = lax.bitcast_convert_type(lax.bitcast_convert_type(z2[t], jnp.int32) | sign, F32)
                sp2 = jnp.maximum(z2[t], 0.0) + jnp.log(1.0 + jnp.exp2(neg_abs)) * INV_LN2
                keep = jnp.where(diag_mask, sp2, 0.0) if t[1] == n else sp2
                r[t] = _dot(keep.astype(BF16), tri_ref[...])
                log_beta[t] = z2[t] - sp2
            rest, acc = [zeros, zeros], [zeros, zeros]
            for hh, j in tiles:
                rt = r[hh, j]
                arg = log_beta[hh, j] - rt[:, :tq] - jnp.concatenate([rest[hh]] * (tq // LANES), axis=1)
                a = jnp.exp2(jnp.minimum(arg, 0.0))
                if j == n:
                    a = jnp.where(diag_mask, a, 0.0)
                acc[hh] = acc[hh] + _dot(a.astype(BF16), v_ref[0, keys[j], :])
                rest[hh] = rest[hh] + rt[:, tq:]
            o_ref[0] = jnp.where(lane < SB_HEAD_DIM, acc[0], acc[1]).astype(o_ref.dtype)
        return run

    for n in range(n_q):
        pl.when(qi == n)(whole_row(n))


def _sb(qkv, tri):
    b, seq, _ = qkv.shape
    n_pairs = SB_HEADS // 2
    tq = min(ATT_BLOCK, seq)
    return pl.pallas_call(
        functools.partial(_sb_kernel, tq=tq, n_q=seq // tq),
        grid=(b, n_pairs, seq // tq),
        in_specs=[pl.BlockSpec((1, tq, LANES), lambda i, j, k: (i, k, j)),
                  pl.BlockSpec((1, seq, LANES), lambda i, j, k: (i, 0, n_pairs + j)),
                  pl.BlockSpec((1, seq, LANES), lambda i, j, k: (i, 0, 2 * n_pairs + j)),
                  _resident(tri.shape)],
        out_specs=pl.BlockSpec((1, tq, LANES), lambda i, j, k: (i, k, j)),
        out_shape=jax.ShapeDtypeStruct((b, seq, SB_HEADS * SB_HEAD_DIM), BF16),
        compiler_params=_params(3),
        name="stickbreak",
    )(qkv, qkv, qkv, tri)


def _hgrn_kernel(f_ref, qig_ref, lbp_ref, nw_ref, tri_ref, o_ref, b_ref, *, seq, layer, unroll):
    width = HG_HEADS * HG_KEY
    lbp = lbp_ref[...]
    e = jnp.exp(lbp - jnp.max(lbp, axis=0, keepdims=True))
    sm = e / jnp.sum(e, axis=0, keepdims=True)
    lb = jnp.zeros((1, width), F32)
    for j in range(1, layer + 1):
        lb = lb + sm[j:j + 1, :]
    lb = jnp.clip(lb, 0.0, 1.0 - 1e-6)
    lb_floor = jnp.maximum(lb, LB_FLOOR)

    gb = tri_ref.shape[0]
    for r in range(seq // gb):
        rows = slice(r * gb, (r + 1) * gb)
        log_f = jnp.minimum(jnp.log(lb_floor + (1.0 - lb) * jax.nn.sigmoid(f_ref[0, rows, :])), 0.0)
        acc = None
        for part in _split_bf16(log_f, 2):
            t = _dot(tri_ref[...], part)
            acc = t if acc is None else acc + t
        b_ref[rows, :] = acc

    n_sub = CHUNK // HG_SUB
    causal = (lax.broadcasted_iota(jnp.int32, (CHUNK, CHUNK), 1)
              <= lax.broadcasted_iota(jnp.int32, (CHUNK, CHUNK), 0))
    row_id = lax.broadcasted_iota(jnp.int32, (CHUNK, 1), 0)

    def step(n, states):
        states = list(states)
        tiles = [(u, h) for u in range(unroll) for h in range(HG_HEADS)]
        rows = {u: pl.ds(pl.multiple_of((n * unroll + u) * CHUNK, CHUNK), CHUNK) for u in range(unroll)}
        cols = {h: slice(h * HG_KEY, (h + 1) * HG_KEY) for h in range(HG_HEADS)}

        work = {}
        for u, h in tiles:
            x = f_ref[0, rows[u], cols[h]]
            b = b_ref[rows[u], cols[h]]
            q_raw = qig_ref[0, rows[u], cols[h]].astype(F32)
            q = q_raw * jax.nn.sigmoid(q_raw)
            k = (1.0 - lb[:, cols[h]]) * jax.nn.sigmoid(-x)
            v16 = qig_ref[0, rows[u], width + h * HG_KEY:width + (h + 1) * HG_KEY]
            b_last = b[CHUNK - 1:CHUNK, :]
            qd = (q * jnp.exp(b)).astype(BF16)
            kd = (k * jnp.exp(b_last - b)).astype(BF16)
            update = lax.dot_general(v16, kd, TN_DIMS, preferred_element_type=F32)
            qf, kf = [], []
            for i in range(n_sub):
                lo, hi = i * HG_SUB, (i + 1) * HG_SUB
                b_ref_i = jnp.zeros((1, HG_KEY), F32) if i == 0 else b[lo - 1:lo, :]
                qf_i = q[lo:hi, :] * jnp.exp(b[lo:hi, :] - b_ref_i)
                above = [jnp.zeros((lo, HG_KEY), F32)] if lo else []
                below = [jnp.zeros((CHUNK - hi, HG_KEY), F32)] if hi < CHUNK else []
                qf.append(jnp.concatenate(above + [qf_i] + below, axis=0))
                kf.append(jnp.where(row_id < hi, k * jnp.exp(b_ref_i - b), 0.0))
            a = lax.dot_general(jnp.concatenate(qf, axis=1).astype(BF16),
                                jnp.concatenate(kf, axis=1).astype(BF16), NT_DIMS,
                                preferred_element_type=F32)
            a = jnp.where(causal, a, 0.0).astype(BF16)
            work[u, h] = (qd, update, jnp.exp(b_last), a, v16)

        inter = {}
        for u, h in tiles:
            qd, update, decay, _, _ = work[u, h]
            inter[u, h] = lax.dot_general(qd, states[h].astype(BF16), NT_DIMS, preferred_element_type=F32)
            states[h] = states[h] * decay + update

        for u, h in tiles:
            _, _, _, a, v16 = work[u, h]
            o = inter[u, h] + _dot(a, v16)
            o = o * lax.rsqrt(jnp.mean(o * o, axis=-1, keepdims=True) + EPS) * nw_ref[:, cols[h]]
            g = qig_ref[0, rows[u], 2 * width + h * HG_KEY:2 * width + (h + 1) * HG_KEY].astype(F32)
            o_ref[0, rows[u], cols[h]] = (o * (g * jax.nn.sigmoid(g))).astype(o_ref.dtype)
        return tuple(states)

    zero = jnp.zeros((HG_KEY, HG_KEY), F32)
    lax.fori_loop(0, seq // (CHUNK * unroll), step, (zero,) * HG_HEADS)


def _hgrn(hg_f, hg_qig, lbp, nw, tri, layer):
    b, seq, width = hg_f.shape
    return pl.pallas_call(
        functools.partial(_hgrn_kernel, seq=seq, layer=layer, unroll=4),
        grid=(b,),
        in_specs=[pl.BlockSpec((1, seq, width), lambda i: (i, 0, 0)),
                  pl.BlockSpec((1, seq, 3 * width), lambda i: (i, 0, 0)),
                  _resident(lbp.shape), _resident(nw.shape), _resident(tri.shape)],
        out_specs=pl.BlockSpec((1, seq, width), lambda i: (i, 0, 0)),
        out_shape=jax.ShapeDtypeStruct((b, seq, width), BF16),
        scratch_shapes=[pltpu.VMEM((seq, width), F32)],
        compiler_params=_params(1),
        name="hgrn2",
    )(hg_f, hg_qig, lbp, nw, tri)


def _merge_kernel(h_ref, ya_ref, yb_ref, yc_ref, gate_ref, wa_ref, wb_ref, wc_ref, wo_ref, o_ref):
    d = h_ref.shape[1]
    merged = None
    for j, (y_ref, w_ref) in enumerate(((ya_ref, wa_ref), (yb_ref, wb_ref), (yc_ref, wc_ref))):
        gate = jax.nn.sigmoid(gate_ref[:, j * d:(j + 1) * d].astype(F32))
        term = gate * _dot(y_ref[...], w_ref[...])
        merged = term if merged is None else merged + term
    o_ref[...] = h_ref[...] + _dot(merged.astype(BF16), wo_ref[...])


def _merge(h, ya, yb, yc, gates, wa, wb, wc, wo):
    t, d = h.shape
    tm = min(ROW_BLOCK, t)
    row = lambda n: pl.BlockSpec((tm, n), lambda i: (i, 0))
    return pl.pallas_call(
        _merge_kernel,
        grid=(t // tm,),
        in_specs=[row(d), row(ya.shape[1]), row(yb.shape[1]), row(yc.shape[1]), row(gates.shape[1]),
                  _resident(wa.shape), _resident(wb.shape), _resident(wc.shape), _resident(wo.shape)],
        out_specs=row(d),
        out_shape=jax.ShapeDtypeStruct((t, d), F32),
        compiler_params=_params(1),
        name="merge",
    )(h, ya, yb, yc, gates, wa, wb, wc, wo)


def _ple_kernel(h_ref, p_ref, nw_ref, wg_ref, wp_ref, fw_ref, o_ref, *, final):
    h = h_ref[...]
    gate = jax.nn.sigmoid(_dot(_rms(h, nw_ref[...]).astype(BF16), wg_ref[...]))
    h = h + _dot(p_ref[...].astype(BF16), wp_ref[...]) * gate
    o_ref[...] = _rms(h, fw_ref[...]) if final else h


def _ple(h, p, nw, wg, wp, fw, final):
    t, d = h.shape
    tm = min(ROW_BLOCK, t)
    row = lambda n: pl.BlockSpec((tm, n), lambda i: (i, 0))
    return pl.pallas_call(
        functools.partial(_ple_kernel, final=final),
        grid=(t // tm,),
        in_specs=[row(d), row(p.shape[1]), _resident((1, d)), _resident(wg.shape), _resident(wp.shape),
                  _resident((1, d))],
        out_specs=row(d),
        out_shape=jax.ShapeDtypeStruct((t, d), F32),
        compiler_params=_params(1),
        name="ple",
    )(h, p, nw, wg, wp, fw)


def _rot_half(w):
    half = MLA_ROPE // 2
    return jnp.concatenate([-w[..., half:], w[..., :half]], axis=-1)


def _strict_lower(n):
    r = lax.broadcasted_iota(jnp.int32, (n, n), 0)
    c = lax.broadcasted_iota(jnp.int32, (n, n), 1)
    return r > c


def kernel(x, p, positions, ffn_a_norm, ffn_a_w_in, ffn_a_w_out, mix_norm, w_in, mla_q_norm, mla_w_uq, mla_kv_norm, mla_w_ukv, hgrn_lower_bounds, hgrn_out_norm, w_br_mla, w_br_sb, w_br_hgrn, w_out, ffn_b_norm, ffn_b_w_in, ffn_b_w_out, ple_norm, w_ple_gate, w_ple_proj, final_norm):
    b, seq, d = x.shape
    depth = ffn_a_norm.shape[0]
    t = b * seq
    bf = lambda a: a.astype(BF16)
    row = lambda a: a.reshape(1, -1).astype(F32)

    tq = min(ATT_BLOCK, seq)
    sb_tri = jnp.concatenate([_strict_lower(tq), jnp.ones((tq, LANES), bool)], axis=1).astype(BF16)
    gb = 4 * CHUNK
    r = lax.broadcasted_iota(jnp.int32, (gb, gb), 0)
    c = lax.broadcasted_iota(jnp.int32, (gb, gb), 1)
    hg_tri = ((c <= r) & (c // CHUNK == r // CHUNK)).astype(BF16)

    half = MLA_ROPE // 2
    inv = ROPE_BASE ** (-jnp.arange(half, dtype=F32) / half)
    inv = jnp.tile(inv, LANES // half).reshape(1, LANES)
    cos, sin = _rope_tables(positions.reshape(b, seq, 1), inv)

    splits = (MLA_Q_LORA, MLA_KV_LORA, MLA_ROPE, 3 * SB_HEADS * SB_HEAD_DIM,
              4 * HG_HEADS * HG_KEY, N_BRANCH * d)
    hg_w = HG_HEADS * HG_KEY
    widths = (MLA_IN_WIDTH, splits[3], hg_w, 3 * hg_w, splits[5])

    h = x.reshape(t, d)
    for i in range(depth):
        h = _ffn(h, row(ffn_a_norm[i]), bf(ffn_a_w_in[i]), bf(ffn_a_w_out[i]))

        w = w_in[i]
        offs = [0]
        for n in splits:
            offs.append(offs[-1] + n)
        w_cq, w_ckv, w_kr, w_sb, w_hg, w_gate = (w[:, offs[j]:offs[j + 1]] for j in range(6))
        n_sbq = SB_HEADS * SB_HEAD_DIM
        w_sb = jnp.concatenate([w_sb[:, :n_sbq] * (SB_HEAD_DIM ** -0.5 * LOG2E), w_sb[:, n_sbq:]], axis=1)
        w_hq, w_hf, w_hi, w_hgate = (w_hg[:, j * hg_w:(j + 1) * hg_w] for j in range(4))
        w_cat = bf(jnp.concatenate([w_cq, w_ckv, jnp.tile(w_kr, (1, 4)), jnp.tile(_rot_half(w_kr), (1, 4)),
                                    w_sb, w_hf, w_hq, w_hi, w_hgate, w_gate], axis=1))
        mla_in, sb_in, hg_f, hg_qig, gates = _inproj(h, row(mix_norm[i]), w_cat, widths,
                                                     (BF16, BF16, F32, BF16, BF16))

        wq = mla_w_uq[i].reshape(MLA_Q_LORA, MLA_HEADS, MLA_NOPE + MLA_ROPE)
        wq_nope = bf(wq[:, :, :MLA_NOPE].reshape(MLA_Q_LORA, -1))
        wq_rope = wq[:, :, MLA_NOPE:]
        wkv = mla_w_ukv[i].reshape(MLA_KV_LORA, MLA_HEADS, MLA_NOPE + MLA_V)
        y_a = _mla(mla_in.reshape(b, seq, -1), cos, sin, row(mla_q_norm[i]), row(mla_kv_norm[i]),
                   wq_nope, bf(wq_rope.reshape(MLA_Q_LORA, -1)), bf(_rot_half(wq_rope).reshape(MLA_Q_LORA, -1)),
                   bf(wkv[:, :, :MLA_NOPE].reshape(MLA_KV_LORA, -1)), bf(wkv[:, :, MLA_NOPE:].reshape(MLA_KV_LORA, -1)))
        y_b = _sb(sb_in.reshape(b, seq, -1), sb_tri)
        y_c = _hgrn(hg_f.reshape(b, seq, -1), hg_qig.reshape(b, seq, -1), hgrn_lower_bounds.astype(F32),
                    row(hgrn_out_norm[i]), hg_tri, i)

        h = _merge(h, y_a.reshape(t, -1), y_b.reshape(t, -1), y_c.reshape(t, -1), gates,
                   bf(w_br_mla[i]), bf(w_br_sb[i]), bf(w_br_hgrn[i]), bf(w_out[i]))
        h = _ffn(h, row(ffn_b_norm[i]), bf(ffn_b_w_in[i]), bf(ffn_b_w_out[i]))
        h = _ple(h, p[i].reshape(t, -1), row(ple_norm[i]), bf(w_ple_gate[i]), bf(w_ple_proj[i]),
                 row(final_norm), final=(i == depth - 1))
    return h.reshape(b, seq, d)
```

```python
import functools

import jax
import jax.numpy as jnp
from jax import lax
from jax.experimental import pallas as pl
from jax.experimental.pallas import tpu as pltpu

F32 = jnp.float32
BF16 = jnp.bfloat16

EPS = 1e-6
LB_FLOOR = 1e-30
CHUNK = 64
HG_SUB = 16
MLA_HEADS = 8
MLA_NOPE = 64
MLA_ROPE = 32
MLA_V = 64
MLA_Q_LORA = 384
MLA_KV_LORA = 256
ROPE_BASE = 10000.0
LOG2E = 1.4426950408889634
INV_LN2 = LOG2E
SB_HEADS = 8
SB_HEAD_DIM = 64
HG_HEADS = 4
HG_KEY = 128
N_BRANCH = 3

LANES = 128
MLA_IN_WIDTH = MLA_Q_LORA + MLA_KV_LORA + 2 * LANES
ATT_BLOCK = 256
ROW_BLOCK = 512
VMEM_LIMIT_BYTES = 56 * 1024 * 1024

NT_DIMS = (((1,), (1,)), ((), ()))
TN_DIMS = (((0,), (0,)), ((), ()))


def _rms(x, w):
    return x * lax.rsqrt(jnp.mean(x * x, axis=-1, keepdims=True) + EPS) * w


def _dot(a, b):
    return jnp.dot(a, b, preferred_element_type=F32)


def _softplus(z):
    return jnp.maximum(z, 0.0) + jnp.log1p(jnp.exp(-jnp.abs(z)))


def _split_bf16(x, terms):
    parts = []
    for _ in range(terms - 1):
        hi = x.astype(BF16)
        parts.append(hi)
        x = x - hi.astype(F32)
    parts.append(x.astype(BF16))
    return parts


def _params(n_axes):
    return pltpu.CompilerParams(dimension_semantics=("arbitrary",) * n_axes,
                                vmem_limit_bytes=VMEM_LIMIT_BYTES)


def _resident(shape):
    return pl.BlockSpec(shape, lambda *_: (0,) * len(shape), pipeline_mode=pl.Buffered(1))


def _ffn_kernel(h_ref, nw_ref, win_ref, wout_ref, *rest, d_ff, fc, embed, final):
    if embed:
        p_ref, pn_ref, wg_ref, wp_ref, fw_ref, o_ref, xn_ref, hid_ref = rest
    else:
        o_ref, xn_ref, hid_ref = rest
    x = h_ref[...]
    xn_ref[...] = _rms(x, nw_ref[...]).astype(BF16)
    for c in range(d_ff // fc):
        xn = xn_ref[...]
        g = _dot(xn, win_ref[:, c * fc:(c + 1) * fc])
        up = _dot(xn, win_ref[:, d_ff + c * fc:d_ff + (c + 1) * fc])
        hid_ref[:, c * fc:(c + 1) * fc] = (g * jax.nn.sigmoid(g) * up).astype(BF16)
    h = x + 0.5 * _dot(hid_ref[...], wout_ref[...])
    if embed:
        gate = jax.nn.sigmoid(_dot(_rms(h, pn_ref[...]).astype(BF16), wg_ref[...]))
        h = h + _dot(p_ref[...].astype(BF16), wp_ref[...]) * gate
        if final:
            h = _rms(h, fw_ref[...])
    o_ref[...] = h


def _ffn(h, nw, w_in, w_out, embed=None, final=False):
    t, d = h.shape
    d_ff = w_out.shape[0]
    tm = min(ROW_BLOCK, t)
    row = lambda n: pl.BlockSpec((tm, n), lambda i: (i, 0))
    in_specs = [row(d), _resident((1, d)), _resident((d, 2 * d_ff)), _resident((d_ff, d))]
    args = [h, nw, w_in, w_out]
    if embed is not None:
        p, pn, wg, wp, fw = embed
        in_specs += [row(p.shape[1]), _resident(pn.shape), _resident(wg.shape), _resident(wp.shape),
                     _resident(fw.shape)]
        args += [p, pn, wg, wp, fw]
    return pl.pallas_call(
        functools.partial(_ffn_kernel, d_ff=d_ff, fc=256, embed=embed is not None, final=final),
        grid=(t // tm,),
        in_specs=in_specs,
        out_specs=row(d),
        out_shape=jax.ShapeDtypeStruct((t, d), F32),
        scratch_shapes=[pltpu.VMEM((tm, d), BF16), pltpu.VMEM((tm, d_ff), BF16)],
        compiler_params=_params(1),
        name="ffn",
    )(*args)


def _inproj_kernel(h_ref, nw_ref, w_ref, *o_refs, widths, nc):
    u = _rms(h_ref[...], nw_ref[...]).astype(BF16)
    start = 0
    for o_ref, width in zip(o_refs, widths):
        for c in range(0, width, nc):
            n = min(nc, width - c)
            o_ref[:, c:c + n] = _dot(u, w_ref[:, start + c:start + c + n]).astype(o_ref.dtype)
        start += width


def _inproj(h, nw, w, widths, dtypes):
    t, d = h.shape
    tm = min(ROW_BLOCK, t)
    return pl.pallas_call(
        functools.partial(_inproj_kernel, widths=widths, nc=256),
        grid=(t // tm,),
        in_specs=[pl.BlockSpec((tm, d), lambda i: (i, 0)), _resident((1, d)), _resident(w.shape)],
        out_specs=[pl.BlockSpec((tm, n), lambda i: (i, 0)) for n in widths],
        out_shape=[jax.ShapeDtypeStruct((t, n), dt) for n, dt in zip(widths, dtypes)],
        compiler_params=_params(1),
        name="inproj",
    )(h, nw, w)


def _rope_kernel(pos_ref, inv_ref, cos_ref, sin_ref):
    ang = pos_ref[0].astype(F32) * inv_ref[...]
    cos_ref[0] = jnp.cos(ang)
    sin_ref[0] = jnp.sin(ang)


def _rope_tables(pos, inv):
    b, seq, _ = pos.shape
    out = jax.ShapeDtypeStruct((b, seq, LANES), F32)
    spec = pl.BlockSpec((1, seq, LANES), lambda i: (i, 0, 0))
    return pl.pallas_call(
        _rope_kernel,
        grid=(b,),
        in_specs=[pl.BlockSpec((1, seq, 1), lambda i: (i, 0, 0)), _resident(inv.shape)],
        out_specs=[spec, spec],
        out_shape=[out, out],
        compiler_params=_params(1),
        name="rope_tables",
    )(pos, inv)


def _mla_kernel(x_ref, cos_ref, sin_ref, qn_ref, kvn_ref, wqn_ref, wqr_ref, wqrp_ref, wk_ref, wv_ref,
                o_ref, qcat_ref, kcat_ref, v_ref, *, seq, tq, scale):
    p = pl.program_id(1)
    n_pairs = MLA_HEADS // 2

    @pl.when(p == 0)
    def _project():
        rb = min(ROW_BLOCK, seq)
        for r in range(seq // rb):
            rows = slice(r * rb, (r + 1) * rb)
            x = x_ref[0, rows, :]
            c_q = x[:, :MLA_Q_LORA].astype(F32)
            c_kv = x[:, MLA_Q_LORA:MLA_Q_LORA + MLA_KV_LORA].astype(F32)
            kr = x[:, MLA_Q_LORA + MLA_KV_LORA:MLA_Q_LORA + MLA_KV_LORA + LANES].astype(F32)
            krp = x[:, MLA_Q_LORA + MLA_KV_LORA + LANES:].astype(F32)
            cos, sin = cos_ref[0, rows, :], sin_ref[0, rows, :]
            cqn = _rms(c_q, qn_ref[...]).astype(BF16)
            ckvn = _rms(c_kv, kvn_ref[...]).astype(BF16)
            k_rope = (kr * cos + krp * sin).astype(BF16)
            for g in range(2):
                cols = slice(g * LANES, (g + 1) * LANES)
                q_rope = ((_dot(cqn, wqr_ref[:, cols]) * cos + _dot(cqn, wqrp_ref[:, cols]) * sin)
                          * scale).astype(BF16)
                qcat_ref[2 * g, rows, LANES:] = q_rope
                qcat_ref[2 * g + 1, rows, LANES:] = q_rope
            for pp in range(n_pairs):
                cols = slice(pp * LANES, (pp + 1) * LANES)
                qcat_ref[pp, rows, :LANES] = (_dot(cqn, wqn_ref[:, cols]) * scale).astype(BF16)
                kcat_ref[pp, rows, :LANES] = _dot(ckvn, wk_ref[:, cols]).astype(BF16)
                kcat_ref[pp, rows, LANES:] = k_rope
                v_ref[pp, rows, :] = _dot(ckvn, wv_ref[:, cols]).astype(BF16)

    lane = lax.broadcasted_iota(jnp.int32, (1, 2 * LANES), 1)
    out_lane = lax.broadcasted_iota(jnp.int32, (1, LANES), 1)
    ri = lax.broadcasted_iota(jnp.int32, (tq, tq), 0)
    ci = lax.broadcasted_iota(jnp.int32, (tq, tq), 1)
    diag_mask = (ci // CHUNK) <= (ri // CHUNK)
    group = jnp.where(lane < LANES, lane // MLA_NOPE, 2 + (lane - LANES) // MLA_ROPE)

    for n in range(seq // tq):
        keys = [slice(j * tq, (j + 1) * tq) for j in range(n + 1)]
        qc = qcat_ref[p, keys[n], :]
        s = []
        for hh in range(2):
            own = (group == hh) | (group == 2 + (p % 2) * 2 + hh)
            qh = qc * jnp.where(own, 1.0, 0.0).astype(BF16)
            sh = [lax.dot_general(qh, kcat_ref[p, kj, :], NT_DIMS, preferred_element_type=F32)
                  for kj in keys]
            sh[n] = jnp.where(diag_mask, sh[n], -jnp.inf)
            s.append(sh)
        m = [jnp.max(functools.reduce(jnp.maximum, sh), axis=-1, keepdims=True) for sh in s]
        outs = []
        for hh in range(2):
            pr = [jnp.exp2(sj - m[hh]) for sj in s[hh]]
            l = jnp.sum(functools.reduce(jnp.add, pr), axis=-1, keepdims=True)
            acc = None
            for j, kj in enumerate(keys):
                t = _dot(pr[j].astype(BF16), v_ref[p, kj, :])
                acc = t if acc is None else acc + t
            outs.append(acc * (1.0 / l))
        o_ref[0, keys[n], :] = jnp.where(out_lane < MLA_V, outs[0], outs[1]).astype(o_ref.dtype)


def _mla(x, cos, sin, qn, kvn, wqn, wqr, wqrp, wk, wv):
    b, seq, _ = x.shape
    n_pairs = MLA_HEADS // 2
    tq = min(ATT_BLOCK, seq)
    scale = float((MLA_NOPE + MLA_ROPE) ** -0.5) * LOG2E
    return pl.pallas_call(
        functools.partial(_mla_kernel, seq=seq, tq=tq, scale=scale),
        grid=(b, n_pairs),
        in_specs=[pl.BlockSpec((1, seq, MLA_IN_WIDTH), lambda i, j: (i, 0, 0)),
                  pl.BlockSpec((1, seq, LANES), lambda i, j: (i, 0, 0)),
                  pl.BlockSpec((1, seq, LANES), lambda i, j: (i, 0, 0)),
                  _resident(qn.shape), _resident(kvn.shape),
                  _resident(wqn.shape), _resident(wqr.shape), _resident(wqrp.shape),
                  _resident(wk.shape), _resident(wv.shape)],
        out_specs=pl.BlockSpec((1, seq, LANES), lambda i, j: (i, 0, j)),
        out_shape=jax.ShapeDtypeStruct((b, seq, MLA_HEADS * MLA_V), BF16),
        scratch_shapes=[pltpu.VMEM((n_pairs, seq, 2 * LANES), BF16),
                        pltpu.VMEM((n_pairs, seq, 2 * LANES), BF16),
                        pltpu.VMEM((n_pairs, seq, LANES), BF16)],
        compiler_params=_params(2),
        name="mla",
    )(x, cos, sin, qn, kvn, wqn, wqr, wqrp, wk, wv)


def _sb_kernel(q_ref, k_ref, v_ref, tri_ref, o_ref, *, tq, n_q):
    lane = lax.broadcasted_iota(jnp.int32, (1, LANES), 1)
    ri = lax.broadcasted_iota(jnp.int32, (tq, tq), 0)
    ci = lax.broadcasted_iota(jnp.int32, (tq, tq), 1)
    diag_mask = ci < ri
    sign = jnp.int32(-2 ** 31)
    keys = [slice(j * tq, (j + 1) * tq) for j in range(n_q)]

    for n in range(n_q):
        q = q_ref[0, keys[n], :]
        qhs = [jnp.where((lane // SB_HEAD_DIM) == hh, q, jnp.zeros_like(q)) for hh in range(2)]
        tiles = [(hh, j) for j in range(n, -1, -1) for hh in range(2)]
        z2 = {t: lax.dot_general(qhs[t[0]], k_ref[0, keys[t[1]], :], NT_DIMS,
                                 preferred_element_type=F32) for t in tiles}
        suffix, total = {}, {}
        for t in tiles:
            neg_abs = lax.bitcast_convert_type(lax.bitcast_convert_type(z2[t], jnp.int32) | sign, F32)
            sp2 = jnp.maximum(z2[t], 0.0) + jnp.log(1.0 + jnp.exp2(neg_abs)) * INV_LN2
            keep = jnp.where(diag_mask, sp2, 0.0) if t[1] == n else sp2
            suffix[t] = _dot(keep.astype(BF16), tri_ref[...])
            total[t] = jnp.sum(keep, axis=-1, keepdims=True)
        rest = [jnp.zeros((tq, 1), F32)] * 2
        acc = [None, None]
        for hh, j in tiles:
            a = jnp.exp2(jnp.minimum(z2[hh, j] - suffix[hh, j] - rest[hh], 0.0))
            if j == n:
                a = jnp.where(diag_mask, a, 0.0)
            pv = _dot(a.astype(BF16), v_ref[0, keys[j], :])
            acc[hh] = pv if acc[hh] is None else acc[hh] + pv
            rest[hh] = rest[hh] + total[hh, j]
        o_ref[0, keys[n], :] = jnp.where(lane < SB_HEAD_DIM, acc[0], acc[1]).astype(o_ref.dtype)


def _sb(qkv, tri):
    b, seq, _ = qkv.shape
    n_pairs = SB_HEADS // 2
    tq = min(ATT_BLOCK, seq)
    col = lambda off: pl.BlockSpec((1, seq, LANES), lambda i, j: (i, 0, off + j))
    return pl.pallas_call(
        functools.partial(_sb_kernel, tq=tq, n_q=seq // tq),
        grid=(b, n_pairs),
        in_specs=[col(0), col(n_pairs), col(2 * n_pairs), _resident(tri.shape)],
        out_specs=col(0),
        out_shape=jax.ShapeDtypeStruct((b, seq, SB_HEADS * SB_HEAD_DIM), BF16),
        compiler_params=_params(2),
        name="stickbreak",
    )(qkv, qkv, qkv, tri)


def _hgrn_kernel(f_ref, qig_ref, lbp_ref, nw_ref, tri_ref, o_ref, b_ref, *, seq, layer, unroll):
    width = HG_HEADS * HG_KEY
    lbp = lbp_ref[...]
    e = jnp.exp(lbp - jnp.max(lbp, axis=0, keepdims=True))
    sm = e / jnp.sum(e, axis=0, keepdims=True)
    lb = jnp.zeros((1, width), F32)
    for j in range(1, layer + 1):
        lb = lb + sm[j:j + 1, :]
    lb = jnp.clip(lb, 0.0, 1.0 - 1e-6)
    lb_floor = jnp.maximum(lb, LB_FLOOR)

    gb = tri_ref.shape[0]
    for r in range(seq // gb):
        rows = slice(r * gb, (r + 1) * gb)
        log_f = jnp.minimum(jnp.log(lb_floor + (1.0 - lb) * jax.nn.sigmoid(f_ref[0, rows, :])), 0.0)
        acc = None
        for part in _split_bf16(log_f, 2):
            t = _dot(tri_ref[...], part)
            acc = t if acc is None else acc + t
        b_ref[rows, :] = acc

    n_sub = CHUNK // HG_SUB
    causal = (lax.broadcasted_iota(jnp.int32, (CHUNK, CHUNK), 1)
              <= lax.broadcasted_iota(jnp.int32, (CHUNK, CHUNK), 0))
    row_id = lax.broadcasted_iota(jnp.int32, (CHUNK, 1), 0)

    def step(n, states):
        states = list(states)
        tiles = [(u, h) for u in range(unroll) for h in range(HG_HEADS)]
        rows = {u: pl.ds(pl.multiple_of((n * unroll + u) * CHUNK, CHUNK), CHUNK) for u in range(unroll)}
        cols = {h: slice(h * HG_KEY, (h + 1) * HG_KEY) for h in range(HG_HEADS)}

        work = {}
        for u, h in tiles:
            x = f_ref[0, rows[u], cols[h]]
            b = b_ref[rows[u], cols[h]]
            q_raw = qig_ref[0, rows[u], cols[h]].astype(F32)
            q = q_raw * jax.nn.sigmoid(q_raw)
            k = (1.0 - lb[:, cols[h]]) * jax.nn.sigmoid(-x)
            v16 = qig_ref[0, rows[u], width + h * HG_KEY:width + (h + 1) * HG_KEY]
            b_last = b[CHUNK - 1:CHUNK, :]
            qd = (q * jnp.exp(b)).astype(BF16)
            kd = (k * jnp.exp(b_last - b)).astype(BF16)
            update = lax.dot_general(v16, kd, TN_DIMS, preferred_element_type=F32)
            qf, kf = [], []
            for i in range(n_sub):
                lo, hi = i * HG_SUB, (i + 1) * HG_SUB
                b_ref_i = jnp.zeros((1, HG_KEY), F32) if i == 0 else b[lo - 1:lo, :]
                qf_i = q[lo:hi, :] * jnp.exp(b[lo:hi, :] - b_ref_i)
                above = [jnp.zeros((lo, HG_KEY), F32)] if lo else []
                below = [jnp.zeros((CHUNK - hi, HG_KEY), F32)] if hi < CHUNK else []
                qf.append(jnp.concatenate(above + [qf_i] + below, axis=0))
                kf.append(jnp.where(row_id < hi, k * jnp.exp(b_ref_i - b), 0.0))
            a = lax.dot_general(jnp.concatenate(qf, axis=1).astype(BF16),
                                jnp.concatenate(kf, axis=1).astype(BF16), NT_DIMS,
                                preferred_element_type=F32)
            a = jnp.where(causal, a, 0.0).astype(BF16)
            work[u, h] = (qd, update, jnp.exp(b_last), a, v16)

        inter = {}
        for u, h in tiles:
            qd, update, decay, _, _ = work[u, h]
            inter[u, h] = lax.dot_general(qd, states[h].astype(BF16), NT_DIMS, preferred_element_type=F32)
            states[h] = states[h] * decay + update

        for u, h in tiles:
            _, _, _, a, v16 = work[u, h]
            o = inter[u, h] + _dot(a, v16)
            o = o * lax.rsqrt(jnp.mean(o * o, axis=-1, keepdims=True) + EPS) * nw_ref[:, cols[h]]
            g = qig_ref[0, rows[u], 2 * width + h * HG_KEY:2 * width + (h + 1) * HG_KEY].astype(F32)
            o_ref[0, rows[u], cols[h]] = (o * (g * jax.nn.sigmoid(g))).astype(o_ref.dtype)
        return tuple(states)

    zero = jnp.zeros((HG_KEY, HG_KEY), F32)
    lax.fori_loop(0, seq // (CHUNK * unroll), step, (zero,) * HG_HEADS)


def _hgrn(hg_f, hg_qig, lbp, nw, tri, layer):
    b, seq, width = hg_f.shape
    return pl.pallas_call(
        functools.partial(_hgrn_kernel, seq=seq, layer=layer, unroll=4),
        grid=(b,),
        in_specs=[pl.BlockSpec((1, seq, width), lambda i: (i, 0, 0)),
                  pl.BlockSpec((1, seq, 3 * width), lambda i: (i, 0, 0)),
                  _resident(lbp.shape), _resident(nw.shape), _resident(tri.shape)],
        out_specs=pl.BlockSpec((1, seq, width), lambda i: (i, 0, 0)),
        out_shape=jax.ShapeDtypeStruct((b, seq, width), BF16),
        scratch_shapes=[pltpu.VMEM((seq, width), F32)],
        compiler_params=_params(1),
        name="hgrn2",
    )(hg_f, hg_qig, lbp, nw, tri)


def _merge_kernel(h_ref, ya_ref, yb_ref, yc_ref, gate_ref, wa_ref, wb_ref, wc_ref, wo_ref, o_ref):
    d = h_ref.shape[1]
    merged = None
    for j, (y_ref, w_ref) in enumerate(((ya_ref, wa_ref), (yb_ref, wb_ref), (yc_ref, wc_ref))):
        gate = jax.nn.sigmoid(gate_ref[:, j * d:(j + 1) * d].astype(F32))
        term = gate * _dot(y_ref[...], w_ref[...])
        merged = term if merged is None else merged + term
    o_ref[...] = h_ref[...] + _dot(merged.astype(BF16), wo_ref[...])


def _merge(h, ya, yb, yc, gates, wa, wb, wc, wo):
    t, d = h.shape
    tm = min(ROW_BLOCK, t)
    row = lambda n: pl.BlockSpec((tm, n), lambda i: (i, 0))
    return pl.pallas_call(
        _merge_kernel,
        grid=(t // tm,),
        in_specs=[row(d), row(ya.shape[1]), row(yb.shape[1]), row(yc.shape[1]), row(gates.shape[1]),
                  _resident(wa.shape), _resident(wb.shape), _resident(wc.shape), _resident(wo.shape)],
        out_specs=row(d),
        out_shape=jax.ShapeDtypeStruct((t, d), F32),
        compiler_params=_params(1),
        name="merge",
    )(h, ya, yb, yc, gates, wa, wb, wc, wo)


def _rot_half(w):
    half = MLA_ROPE // 2
    return jnp.concatenate([-w[..., half:], w[..., :half]], axis=-1)


def _lower(n):
    r = lax.broadcasted_iota(jnp.int32, (n, n), 0)
    c = lax.broadcasted_iota(jnp.int32, (n, n), 1)
    return r >= c


def kernel(x, p, positions, ffn_a_norm, ffn_a_w_in, ffn_a_w_out, mix_norm, w_in, mla_q_norm, mla_w_uq, mla_kv_norm, mla_w_ukv, hgrn_lower_bounds, hgrn_out_norm, w_br_mla, w_br_sb, w_br_hgrn, w_out, ffn_b_norm, ffn_b_w_in, ffn_b_w_out, ple_norm, w_ple_gate, w_ple_proj, final_norm):
    b, seq, d = x.shape
    depth = ffn_a_norm.shape[0]
    t = b * seq
    bf = lambda a: a.astype(BF16)
    row = lambda a: a.reshape(1, -1).astype(F32)

    tq = min(ATT_BLOCK, seq)
    sb_tri = _lower(tq).astype(BF16)
    gb = 4 * CHUNK
    r = lax.broadcasted_iota(jnp.int32, (gb, gb), 0)
    c = lax.broadcasted_iota(jnp.int32, (gb, gb), 1)
    hg_tri = ((c <= r) & (c // CHUNK == r // CHUNK)).astype(BF16)

    half = MLA_ROPE // 2
    inv = ROPE_BASE ** (-jnp.arange(half, dtype=F32) / half)
    inv = jnp.tile(inv, LANES // half).reshape(1, LANES)
    cos, sin = _rope_tables(positions.reshape(b, seq, 1), inv)

    splits = (MLA_Q_LORA, MLA_KV_LORA, MLA_ROPE, 3 * SB_HEADS * SB_HEAD_DIM,
              4 * HG_HEADS * HG_KEY, N_BRANCH * d)
    hg_w = HG_HEADS * HG_KEY
    widths = (MLA_IN_WIDTH, splits[3], hg_w, 3 * hg_w, splits[5])

    h = x.reshape(t, d)
    for i in range(depth):
        h = _ffn(h, row(ffn_a_norm[i]), bf(ffn_a_w_in[i]), bf(ffn_a_w_out[i]))

        w = w_in[i]
        offs = [0]
        for n in splits:
            offs.append(offs[-1] + n)
        w_cq, w_ckv, w_kr, w_sb, w_hg, w_gate = (w[:, offs[j]:offs[j + 1]] for j in range(6))
        n_sbq = SB_HEADS * SB_HEAD_DIM
        w_sb = jnp.concatenate([w_sb[:, :n_sbq] * (SB_HEAD_DIM ** -0.5 * LOG2E), w_sb[:, n_sbq:]], axis=1)
        w_hq, w_hf, w_hi, w_hgate = (w_hg[:, j * hg_w:(j + 1) * hg_w] for j in range(4))
        w_cat = bf(jnp.concatenate([w_cq, w_ckv, jnp.tile(w_kr, (1, 4)), jnp.tile(_rot_half(w_kr), (1, 4)),
                                    w_sb, w_hf, w_hq, w_hi, w_hgate, w_gate], axis=1))
        mla_in, sb_in, hg_f, hg_qig, gates = _inproj(h, row(mix_norm[i]), w_cat, widths,
                                                     (BF16, BF16, F32, BF16, BF16))

        wq = mla_w_uq[i].reshape(MLA_Q_LORA, MLA_HEADS, MLA_NOPE + MLA_ROPE)
        wq_nope = bf(wq[:, :, :MLA_NOPE].reshape(MLA_Q_LORA, -1))
        wq_rope = wq[:, :, MLA_NOPE:]
        wkv = mla_w_ukv[i].reshape(MLA_KV_LORA, MLA_HEADS, MLA_NOPE + MLA_V)
        y_a = _mla(mla_in.reshape(b, seq, -1), cos, sin, row(mla_q_norm[i]), row(mla_kv_norm[i]),
                   wq_nope, bf(wq_rope.reshape(MLA_Q_LORA, -1)), bf(_rot_half(wq_rope).reshape(MLA_Q_LORA, -1)),
                   bf(wkv[:, :, :MLA_NOPE].reshape(MLA_KV_LORA, -1)), bf(wkv[:, :, MLA_NOPE:].reshape(MLA_KV_LORA, -1)))
        y_b = _sb(sb_in.reshape(b, seq, -1), sb_tri)
        y_c = _hgrn(hg_f.reshape(b, seq, -1), hg_qig.reshape(b, seq, -1), hgrn_lower_bounds.astype(F32),
                    row(hgrn_out_norm[i]), hg_tri, i)

        h = _merge(h, y_a.reshape(t, -1), y_b.reshape(t, -1), y_c.reshape(t, -1), gates,
                   bf(w_br_mla[i]), bf(w_br_sb[i]), bf(w_br_hgrn[i]), bf(w_out[i]))
        h = _ffn(h, row(ffn_b_norm[i]), bf(ffn_b_w_in[i]), bf(ffn_b_w_out[i]),
                 embed=(p[i].reshape(t, -1), row(ple_norm[i]), bf(w_ple_gate[i]), bf(w_ple_proj[i]), row(final_norm)),
                 final=(i == depth - 1))
    return h.reshape(b, seq, d)
```

```python
import functools

import jax
import jax.numpy as jnp
from jax import lax
from jax.experimental import pallas as pl
from jax.experimental.pallas import tpu as pltpu

F32 = jnp.float32
BF16 = jnp.bfloat16

EPS = 1e-6
LB_FLOOR = 1e-30
CHUNK = 64
HG_SUB = 16
MLA_HEADS = 8
MLA_NOPE = 64
MLA_ROPE = 32
MLA_V = 64
MLA_Q_LORA = 384
MLA_KV_LORA = 256
ROPE_BASE = 10000.0
LOG2E = 1.4426950408889634
INV_LN2 = LOG2E
SB_DEAD_LOG2 = 160.0
SB_HEADS = 8
SB_HEAD_DIM = 64
HG_HEADS = 4
HG_KEY = 128
N_BRANCH = 3

LANES = 128
MLA_IN_WIDTH = MLA_Q_LORA + MLA_KV_LORA + 2 * LANES
ATT_BLOCK = 256
ROW_BLOCK = 512
CAST_ROWS = 256
VMEM_LIMIT_BYTES = 56 * 1024 * 1024

NT_DIMS = (((1,), (1,)), ((), ()))
TN_DIMS = (((0,), (0,)), ((), ()))


def _rms(x, w):
    return x * lax.rsqrt(jnp.mean(x * x, axis=-1, keepdims=True) + EPS) * w


def _dot(a, b):
    return jnp.dot(a, b, preferred_element_type=F32)


def _softplus(z):
    return jnp.maximum(z, 0.0) + jnp.log1p(jnp.exp(-jnp.abs(z)))


def _split_bf16(x, terms):
    parts = []
    for _ in range(terms - 1):
        hi = x.astype(BF16)
        parts.append(hi)
        x = x - hi.astype(F32)
    parts.append(x.astype(BF16))
    return parts


def _params(n_axes):
    return pltpu.CompilerParams(dimension_semantics=("arbitrary",) * n_axes,
                                vmem_limit_bytes=VMEM_LIMIT_BYTES)


def _resident(shape):
    return pl.BlockSpec(shape, lambda *_: (0,) * len(shape), pipeline_mode=pl.Buffered(1))


def _cast_kernel(w_ref, o_ref):
    o_ref[...] = w_ref[0].astype(o_ref.dtype)


def _layer_bf16(w, layer):
    _, k, n = w.shape
    tk = CAST_ROWS if k % CAST_ROWS == 0 else k
    return pl.pallas_call(
        _cast_kernel,
        grid=(k // tk,),
        in_specs=[pl.BlockSpec((1, tk, n), lambda r: (layer, r, 0))],
        out_specs=pl.BlockSpec((tk, n), lambda r: (r, 0)),
        out_shape=jax.ShapeDtypeStruct((k, n), BF16),
        compiler_params=_params(1),
        name="to_bf16",
    )(w)


def _ffn_kernel(h_ref, nw_ref, win_ref, wout_ref, *rest, d_ff, fc, embed, final):
    if embed:
        p_ref, pn_ref, wg_ref, wp_ref, fw_ref, o_ref, xn_ref, hid_ref = rest
    else:
        o_ref, xn_ref, hid_ref = rest
    x = h_ref[...]
    xn_ref[...] = _rms(x, nw_ref[...]).astype(BF16)
    for c in range(d_ff // fc):
        xn = xn_ref[...]
        g = _dot(xn, win_ref[:, c * fc:(c + 1) * fc])
        up = _dot(xn, win_ref[:, d_ff + c * fc:d_ff + (c + 1) * fc])
        hid_ref[:, c * fc:(c + 1) * fc] = (g * jax.nn.sigmoid(g) * up).astype(BF16)
    h = x + 0.5 * _dot(hid_ref[...], wout_ref[...])
    if embed:
        gate = jax.nn.sigmoid(_dot(_rms(h, pn_ref[...]).astype(BF16), wg_ref[...]))
        h = h + _dot(p_ref[0].astype(BF16), wp_ref[...]) * gate
        if final:
            h = _rms(h, fw_ref[...])
    o_ref[...] = h


def _ffn(h, nw, w_in, w_out, embed=None, final=False):
    t, d = h.shape
    d_ff = w_out.shape[0]
    tm = min(ROW_BLOCK, t)
    row = lambda n: pl.BlockSpec((tm, n), lambda i: (i, 0))
    in_specs = [row(d), _resident((1, d)), _resident((d, 2 * d_ff)), _resident((d_ff, d))]
    args = [h, nw, w_in, w_out]
    if embed is not None:
        p, layer, pn, wg, wp, fw = embed
        in_specs += [pl.BlockSpec((1, tm, p.shape[2]), lambda i: (layer, i, 0)),
                     _resident(pn.shape), _resident(wg.shape), _resident(wp.shape), _resident(fw.shape)]
        args += [p, pn, wg, wp, fw]
    return pl.pallas_call(
        functools.partial(_ffn_kernel, d_ff=d_ff, fc=256, embed=embed is not None, final=final),
        grid=(t // tm,),
        in_specs=in_specs,
        out_specs=row(d),
        out_shape=jax.ShapeDtypeStruct((t, d), F32),
        scratch_shapes=[pltpu.VMEM((tm, d), BF16), pltpu.VMEM((tm, d_ff), BF16)],
        compiler_params=_params(1),
        name="ffn",
    )(*args)


def _inproj_kernel(h_ref, nw_ref, w_ref, *o_refs, widths, nc):
    u = _rms(h_ref[...], nw_ref[...]).astype(BF16)
    start = 0
    for o_ref, width in zip(o_refs, widths):
        for c in range(0, width, nc):
            n = min(nc, width - c)
            o_ref[:, c:c + n] = _dot(u, w_ref[:, start + c:start + c + n]).astype(o_ref.dtype)
        start += width


def _inproj(h, nw, w, widths, dtypes):
    t, d = h.shape
    tm = min(ROW_BLOCK, t)
    return pl.pallas_call(
        functools.partial(_inproj_kernel, widths=widths, nc=256),
        grid=(t // tm,),
        in_specs=[pl.BlockSpec((tm, d), lambda i: (i, 0)), _resident((1, d)), _resident(w.shape)],
        out_specs=[pl.BlockSpec((tm, n), lambda i: (i, 0)) for n in widths],
        out_shape=[jax.ShapeDtypeStruct((t, n), dt) for n, dt in zip(widths, dtypes)],
        compiler_params=_params(1),
        name="inproj",
    )(h, nw, w)


def _rope_kernel(pos_ref, inv_ref, cos_ref, sin_ref):
    ang = pos_ref[0].astype(F32) * inv_ref[...]
    cos_ref[0] = jnp.cos(ang)
    sin_ref[0] = jnp.sin(ang)


def _rope_tables(pos, inv):
    b, seq, _ = pos.shape
    out = jax.ShapeDtypeStruct((b, seq, LANES), F32)
    spec = pl.BlockSpec((1, seq, LANES), lambda i: (i, 0, 0))
    return pl.pallas_call(
        _rope_kernel,
        grid=(b,),
        in_specs=[pl.BlockSpec((1, seq, 1), lambda i: (i, 0, 0)), _resident(inv.shape)],
        out_specs=[spec, spec],
        out_shape=[out, out],
        compiler_params=_params(1),
        name="rope_tables",
    )(pos, inv)


def _mla_kernel(x_ref, cos_ref, sin_ref, qn_ref, kvn_ref, wqn_ref, wqr_ref, wqrp_ref, wk_ref, wv_ref,
                o_ref, qcat_ref, kcat_ref, v_ref, *, seq, tq, scale):
    p = pl.program_id(1)
    n_pairs = MLA_HEADS // 2

    @pl.when(p == 0)
    def _project():
        rb = min(ROW_BLOCK, seq)
        for r in range(seq // rb):
            rows = slice(r * rb, (r + 1) * rb)
            x = x_ref[0, rows, :]
            c_q = x[:, :MLA_Q_LORA].astype(F32)
            c_kv = x[:, MLA_Q_LORA:MLA_Q_LORA + MLA_KV_LORA].astype(F32)
            kr = x[:, MLA_Q_LORA + MLA_KV_LORA:MLA_Q_LORA + MLA_KV_LORA + LANES].astype(F32)
            krp = x[:, MLA_Q_LORA + MLA_KV_LORA + LANES:].astype(F32)
            cos, sin = cos_ref[0, rows, :], sin_ref[0, rows, :]
            cqn = _rms(c_q, qn_ref[...]).astype(BF16)
            ckvn = _rms(c_kv, kvn_ref[...]).astype(BF16)
            k_rope = (kr * cos + krp * sin).astype(BF16)
            halves = lambda a: (a[:, :LANES], a[:, LANES:])
            ropes = zip(halves(_dot(cqn, wqr_ref[...])), halves(_dot(cqn, wqrp_ref[...])))
            for g, (qr, qrp) in enumerate(ropes):
                q_rope = ((qr * cos + qrp * sin) * scale).astype(BF16)
                qcat_ref[2 * g, rows, LANES:] = q_rope
                qcat_ref[2 * g + 1, rows, LANES:] = q_rope
            for g in range(n_pairs // 2):
                cols = slice(2 * g * LANES, (2 * g + 2) * LANES)
                parts = zip(halves(_dot(cqn, wqn_ref[:, cols])), halves(_dot(ckvn, wk_ref[:, cols])),
                            halves(_dot(ckvn, wv_ref[:, cols])))
                for pp, (qn, kn, vv) in enumerate(parts, start=2 * g):
                    qcat_ref[pp, rows, :LANES] = (qn * scale).astype(BF16)
                    kcat_ref[pp, rows, :LANES] = kn.astype(BF16)
                    kcat_ref[pp, rows, LANES:] = k_rope
                    v_ref[pp, rows, :] = vv.astype(BF16)

    lane = lax.broadcasted_iota(jnp.int32, (1, 2 * LANES), 1)
    out_lane = lax.broadcasted_iota(jnp.int32, (1, LANES), 1)
    ri = lax.broadcasted_iota(jnp.int32, (tq, tq), 0)
    ci = lax.broadcasted_iota(jnp.int32, (tq, tq), 1)
    diag_mask = (ci // CHUNK) <= (ri // CHUNK)
    group = jnp.where(lane < LANES, lane // MLA_NOPE, 2 + (lane - LANES) // MLA_ROPE)

    for n in range(seq // tq):
        keys = [slice(j * tq, (j + 1) * tq) for j in range(n + 1)]
        qc = qcat_ref[p, keys[n], :]
        s = []
        for hh in range(2):
            own = (group == hh) | (group == 2 + (p % 2) * 2 + hh)
            qh = qc * jnp.where(own, 1.0, 0.0).astype(BF16)
            sh = [lax.dot_general(qh, kcat_ref[p, kj, :], NT_DIMS, preferred_element_type=F32)
                  for kj in keys]
            sh[n] = jnp.where(diag_mask, sh[n], -jnp.inf)
            s.append(sh)
        m = [jnp.max(functools.reduce(jnp.maximum, sh), axis=-1, keepdims=True) for sh in s]
        outs = []
        for hh in range(2):
            pr = [jnp.exp2(sj - m[hh]) for sj in s[hh]]
            l = jnp.sum(functools.reduce(jnp.add, pr), axis=-1, keepdims=True)
            acc = None
            for j, kj in enumerate(keys):
                t = _dot(pr[j].astype(BF16), v_ref[p, kj, :])
                acc = t if acc is None else acc + t
            outs.append(acc * (1.0 / l))
        o_ref[0, keys[n], :] = jnp.where(out_lane < MLA_V, outs[0], outs[1]).astype(o_ref.dtype)


def _mla(x, cos, sin, qn, kvn, wqn, wqr, wqrp, wk, wv):
    b, seq, _ = x.shape
    n_pairs = MLA_HEADS // 2
    tq = min(ATT_BLOCK, seq)
    scale = float((MLA_NOPE + MLA_ROPE) ** -0.5) * LOG2E
    return pl.pallas_call(
        functools.partial(_mla_kernel, seq=seq, tq=tq, scale=scale),
        grid=(b, n_pairs),
        in_specs=[pl.BlockSpec((1, seq, MLA_IN_WIDTH), lambda i, j: (i, 0, 0)),
                  pl.BlockSpec((1, seq, LANES), lambda i, j: (i, 0, 0)),
                  pl.BlockSpec((1, seq, LANES), lambda i, j: (i, 0, 0)),
                  _resident(qn.shape), _resident(kvn.shape),
                  _resident(wqn.shape), _resident(wqr.shape), _resident(wqrp.shape),
                  _resident(wk.shape), _resident(wv.shape)],
        out_specs=pl.BlockSpec((1, seq, LANES), lambda i, j: (i, 0, j)),
        out_shape=jax.ShapeDtypeStruct((b, seq, MLA_HEADS * MLA_V), BF16),
        scratch_shapes=[pltpu.VMEM((n_pairs, seq, 2 * LANES), BF16),
                        pltpu.VMEM((n_pairs, seq, 2 * LANES), BF16),
                        pltpu.VMEM((n_pairs, seq, LANES), BF16)],
        compiler_params=_params(2),
        name="mla",
    )(x, cos, sin, qn, kvn, wqn, wqr, wqrp, wk, wv)


def _sb_kernel(q_ref, k_ref, v_ref, tri_ref, o_ref, *, tq, n_q):
    lane = lax.broadcasted_iota(jnp.int32, (1, LANES), 1)
    ri = lax.broadcasted_iota(jnp.int32, (tq, tq), 0)
    ci = lax.broadcasted_iota(jnp.int32, (tq, tq), 1)
    diag_mask = ci < ri
    sign = jnp.int32(-2 ** 31)
    rows = [slice(j * tq, (j + 1) * tq) for j in range(n_q)]

    def sweep(blocks, state):
        tiles = [(n, hh, j) for n, js in blocks.items() for j in js for hh in range(2)]
        qhs = {}
        for n in blocks:
            q = q_ref[0, rows[n], :]
            for hh in range(2):
                qhs[n, hh] = jnp.where((lane // SB_HEAD_DIM) == hh, q, jnp.zeros_like(q))
        z2 = {t: lax.dot_general(qhs[t[0], t[1]], k_ref[0, rows[t[2]], :], NT_DIMS,
                                 preferred_element_type=F32) for t in tiles}
        suffix, total = {}, {}
        for t in tiles:
            neg_abs = lax.bitcast_convert_type(lax.bitcast_convert_type(z2[t], jnp.int32) | sign, F32)
            sp2 = jnp.maximum(z2[t], 0.0) + jnp.log(1.0 + jnp.exp2(neg_abs)) * INV_LN2
            keep = jnp.where(diag_mask, sp2, 0.0) if t[2] == t[0] else sp2
            suffix[t] = _dot(keep.astype(BF16), tri_ref[...])
            total[t] = jnp.sum(keep, axis=-1, keepdims=True)
        state = dict(state)
        for n, hh, j in tiles:
            rest, acc = state[n, hh]
            a = jnp.exp2(jnp.minimum(z2[n, hh, j] - suffix[n, hh, j] - rest, 0.0))
            if j == n:
                a = jnp.where(diag_mask, a, 0.0)
            pv = _dot(a.astype(BF16), v_ref[0, rows[j], :])
            state[n, hh] = (rest + total[n, hh, j], pv if acc is None else acc + pv)
        return state

    def store(n, state):
        o_ref[0, rows[n], :] = jnp.where(lane < SB_HEAD_DIM, state[n, 0][1], state[n, 1][1]).astype(o_ref.dtype)

    start = {(n, hh): (jnp.zeros((tq, 1), F32), None) for n in range(n_q) for hh in range(2)}
    near = sweep({n: [j for j in (n, n - 1) if j >= 0] for n in range(n_q)}, start)
    for n in range(n_q):
        store(n, near)

    for n in range(2, n_q):
        alive = jnp.minimum(jnp.min(near[n, 0][0]), jnp.min(near[n, 1][0])) < SB_DEAD_LOG2

        @pl.when(alive)
        def _(n=n):
            far = sweep({n: list(range(n - 2, -1, -1))}, {k: v for k, v in near.items() if k[0] == n})
            store(n, far)


def _sb(qkv, tri):
    b, seq, _ = qkv.shape
    n_pairs = SB_HEADS // 2
    tq = min(ATT_BLOCK, seq)
    col = lambda off: pl.BlockSpec((1, seq, LANES), lambda i, j: (i, 0, off + j))
    return pl.pallas_call(
        functools.partial(_sb_kernel, tq=tq, n_q=seq // tq),
        grid=(b, n_pairs),
        in_specs=[col(0), col(n_pairs), col(2 * n_pairs), _resident(tri.shape)],
        out_specs=col(0),
        out_shape=jax.ShapeDtypeStruct((b, seq, SB_HEADS * SB_HEAD_DIM), BF16),
        compiler_params=_params(2),
        name="stickbreak",
    )(qkv, qkv, qkv, tri)


def _hgrn_kernel(f_ref, qig_ref, lbp_ref, nw_ref, tri_ref, o_ref, b_ref, *, seq, layer, unroll):
    width = HG_HEADS * HG_KEY
    lbp = lbp_ref[...]
    e = jnp.exp(lbp - jnp.max(lbp, axis=0, keepdims=True))
    sm = e / jnp.sum(e, axis=0, keepdims=True)
    lb = jnp.zeros((1, width), F32)
    for j in range(1, layer + 1):
        lb = lb + sm[j:j + 1, :]
    lb = jnp.clip(lb, 0.0, 1.0 - 1e-6)
    lb_floor = jnp.maximum(lb, LB_FLOOR)

    gb = tri_ref.shape[0]
    for r in range(seq // gb):
        rows = slice(r * gb, (r + 1) * gb)
        log_f = jnp.minimum(jnp.log(lb_floor + (1.0 - lb) * jax.nn.sigmoid(f_ref[0, rows, :])), 0.0)
        acc = None
        for part in _split_bf16(log_f, 2):
            t = _dot(tri_ref[...], part)
            acc = t if acc is None else acc + t
        b_ref[rows, :] = acc

    n_sub = CHUNK // HG_SUB
    causal = (lax.broadcasted_iota(jnp.int32, (CHUNK, CHUNK), 1)
              <= lax.broadcasted_iota(jnp.int32, (CHUNK, CHUNK), 0))
    row_id = lax.broadcasted_iota(jnp.int32, (CHUNK, 1), 0)

    def step(n, states):
        states = list(states)
        tiles = [(u, h) for u in range(unroll) for h in range(HG_HEADS)]
        rows = {u: pl.ds(pl.multiple_of((n * unroll + u) * CHUNK, CHUNK), CHUNK) for u in range(unroll)}
        cols = {h: slice(h * HG_KEY, (h + 1) * HG_KEY) for h in range(HG_HEADS)}

        work = {}
        for u, h in tiles:
            x = f_ref[0, rows[u], cols[h]]
            b = b_ref[rows[u], cols[h]]
            q_raw = qig_ref[0, rows[u], cols[h]].astype(F32)
            q = q_raw * jax.nn.sigmoid(q_raw)
            k = (1.0 - lb[:, cols[h]]) * jax.nn.sigmoid(-x)
            v16 = qig_ref[0, rows[u], width + h * HG_KEY:width + (h + 1) * HG_KEY]
            b_last = b[CHUNK - 1:CHUNK, :]
            qd = (q * jnp.exp(b)).astype(BF16)
            kd = (k * jnp.exp(b_last - b)).astype(BF16)
            update = lax.dot_general(v16, kd, TN_DIMS, preferred_element_type=F32)
            qf, kf = [], []
            for i in range(n_sub):
                lo, hi = i * HG_SUB, (i + 1) * HG_SUB
                b_ref_i = jnp.zeros((1, HG_KEY), F32) if i == 0 else b[lo - 1:lo, :]
                qf_i = q[lo:hi, :] * jnp.exp(b[lo:hi, :] - b_ref_i)
                above = [jnp.zeros((lo, HG_KEY), F32)] if lo else []
                below = [jnp.zeros((CHUNK - hi, HG_KEY), F32)] if hi < CHUNK else []
                qf.append(jnp.concatenate(above + [qf_i] + below, axis=0))
                kf.append(jnp.where(row_id < hi, k * jnp.exp(b_ref_i - b), 0.0))
            a = lax.dot_general(jnp.concatenate(qf, axis=1).astype(BF16),
                                jnp.concatenate(kf, axis=1).astype(BF16), NT_DIMS,
                                preferred_element_type=F32)
            a = jnp.where(causal, a, 0.0).astype(BF16)
            work[u, h] = (qd, update, jnp.exp(b_last), a, v16)

        inter = {}
        for u, h in tiles:
            qd, update, decay, _, _ = work[u, h]
            inter[u, h] = lax.dot_general(qd, states[h].astype(BF16), NT_DIMS, preferred_element_type=F32)
            states[h] = states[h] * decay + update

        for u, h in tiles:
            _, _, _, a, v16 = work[u, h]
            o = inter[u, h] + _dot(a, v16)
            o = o * lax.rsqrt(jnp.mean(o * o, axis=-1, keepdims=True) + EPS) * nw_ref[:, cols[h]]
            g = qig_ref[0, rows[u], 2 * width + h * HG_KEY:2 * width + (h + 1) * HG_KEY].astype(F32)
            o_ref[0, rows[u], cols[h]] = (o * (g * jax.nn.sigmoid(g))).astype(o_ref.dtype)
        return tuple(states)

    zero = jnp.zeros((HG_KEY, HG_KEY), F32)
    lax.fori_loop(0, seq // (CHUNK * unroll), step, (zero,) * HG_HEADS)


def _hgrn(hg_f, hg_qig, lbp, nw, tri, layer):
    b, seq, width = hg_f.shape
    return pl.pallas_call(
        functools.partial(_hgrn_kernel, seq=seq, layer=layer, unroll=4),
        grid=(b,),
        in_specs=[pl.BlockSpec((1, seq, width), lambda i: (i, 0, 0)),
                  pl.BlockSpec((1, seq, 3 * width), lambda i: (i, 0, 0)),
                  _resident(lbp.shape), _resident(nw.shape), _resident(tri.shape)],
        out_specs=pl.BlockSpec((1, seq, width), lambda i: (i, 0, 0)),
        out_shape=jax.ShapeDtypeStruct((b, seq, width), BF16),
        scratch_shapes=[pltpu.VMEM((seq, width), F32)],
        compiler_params=_params(1),
        name="hgrn2",
    )(hg_f, hg_qig, lbp, nw, tri)


def _merge_kernel(h_ref, ya_ref, yb_ref, yc_ref, gate_ref, wa_ref, wb_ref, wc_ref, wo_ref, o_ref):
    d = h_ref.shape[1]
    merged = None
    for j, (y_ref, w_ref) in enumerate(((ya_ref, wa_ref), (yb_ref, wb_ref), (yc_ref, wc_ref))):
        gate = jax.nn.sigmoid(gate_ref[:, j * d:(j + 1) * d].astype(F32))
        term = gate * _dot(y_ref[...], w_ref[...])
        merged = term if merged is None else merged + term
    o_ref[...] = h_ref[...] + _dot(merged.astype(BF16), wo_ref[...])


def _merge(h, ya, yb, yc, gates, wa, wb, wc, wo):
    t, d = h.shape
    tm = min(ROW_BLOCK, t)
    row = lambda n: pl.BlockSpec((tm, n), lambda i: (i, 0))
    return pl.pallas_call(
        _merge_kernel,
        grid=(t // tm,),
        in_specs=[row(d), row(ya.shape[1]), row(yb.shape[1]), row(yc.shape[1]), row(gates.shape[1]),
                  _resident(wa.shape), _resident(wb.shape), _resident(wc.shape), _resident(wo.shape)],
        out_specs=row(d),
        out_shape=jax.ShapeDtypeStruct((t, d), F32),
        compiler_params=_params(1),
        name="merge",
    )(h, ya, yb, yc, gates, wa, wb, wc, wo)


def _rot_half(w):
    half = MLA_ROPE // 2
    return jnp.concatenate([-w[..., half:], w[..., :half]], axis=-1)


def _lower(n):
    r = lax.broadcasted_iota(jnp.int32, (n, n), 0)
    c = lax.broadcasted_iota(jnp.int32, (n, n), 1)
    return r >= c


def kernel(x, p, positions, ffn_a_norm, ffn_a_w_in, ffn_a_w_out, mix_norm, w_in, mla_q_norm, mla_w_uq, mla_kv_norm, mla_w_ukv, hgrn_lower_bounds, hgrn_out_norm, w_br_mla, w_br_sb, w_br_hgrn, w_out, ffn_b_norm, ffn_b_w_in, ffn_b_w_out, ple_norm, w_ple_gate, w_ple_proj, final_norm):
    b, seq, d = x.shape
    depth = ffn_a_norm.shape[0]
    t = b * seq
    bf = lambda a: a.astype(BF16)
    row = lambda a: a.reshape(1, -1).astype(F32)

    tq = min(ATT_BLOCK, seq)
    sb_tri = _lower(tq).astype(BF16)
    gb = 4 * CHUNK
    r = lax.broadcasted_iota(jnp.int32, (gb, gb), 0)
    c = lax.broadcasted_iota(jnp.int32, (gb, gb), 1)
    hg_tri = ((c <= r) & (c // CHUNK == r // CHUNK)).astype(BF16)

    half = MLA_ROPE // 2
    inv = ROPE_BASE ** (-jnp.arange(half, dtype=F32) / half)
    inv = jnp.tile(inv, LANES // half).reshape(1, LANES)
    cos, sin = _rope_tables(positions.reshape(b, seq, 1), inv)

    splits = (MLA_Q_LORA, MLA_KV_LORA, MLA_ROPE, 3 * SB_HEADS * SB_HEAD_DIM,
              4 * HG_HEADS * HG_KEY, N_BRANCH * d)
    hg_w = HG_HEADS * HG_KEY
    widths = (MLA_IN_WIDTH, splits[3], hg_w, 3 * hg_w, splits[5])

    h = x.reshape(t, d)
    for i in range(depth):
        h = _ffn(h, row(ffn_a_norm[i]), _layer_bf16(ffn_a_w_in, i), _layer_bf16(ffn_a_w_out, i))

        w = w_in[i]
        offs = [0]
        for n in splits:
            offs.append(offs[-1] + n)
        w_cq, w_ckv, w_kr, w_sb, w_hg, w_gate = (w[:, offs[j]:offs[j + 1]] for j in range(6))
        n_sbq = SB_HEADS * SB_HEAD_DIM
        w_sb = jnp.concatenate([w_sb[:, :n_sbq] * (SB_HEAD_DIM ** -0.5 * LOG2E), w_sb[:, n_sbq:]], axis=1)
        w_hq, w_hf, w_hi, w_hgate = (w_hg[:, j * hg_w:(j + 1) * hg_w] for j in range(4))
        w_cat = bf(jnp.concatenate([w_cq, w_ckv, jnp.tile(w_kr, (1, 4)), jnp.tile(_rot_half(w_kr), (1, 4)),
                                    w_sb, w_hf, w_hq, w_hi, w_hgate, w_gate], axis=1))
        mla_in, sb_in, hg_f, hg_qig, gates = _inproj(h, row(mix_norm[i]), w_cat, widths,
                                                     (BF16, BF16, F32, BF16, BF16))

        wq = mla_w_uq[i].reshape(MLA_Q_LORA, MLA_HEADS, MLA_NOPE + MLA_ROPE)
        wq_nope = bf(wq[:, :, :MLA_NOPE].reshape(MLA_Q_LORA, -1))
        wq_rope = wq[:, :, MLA_NOPE:]
        wkv = mla_w_ukv[i].reshape(MLA_KV_LORA, MLA_HEADS, MLA_NOPE + MLA_V)
        y_a = _mla(mla_in.reshape(b, seq, -1), cos, sin, row(mla_q_norm[i]), row(mla_kv_norm[i]),
                   wq_nope, bf(wq_rope.reshape(MLA_Q_LORA, -1)), bf(_rot_half(wq_rope).reshape(MLA_Q_LORA, -1)),
                   bf(wkv[:, :, :MLA_NOPE].reshape(MLA_KV_LORA, -1)), bf(wkv[:, :, MLA_NOPE:].reshape(MLA_KV_LORA, -1)))
        y_b = _sb(sb_in.reshape(b, seq, -1), sb_tri)
        y_c = _hgrn(hg_f.reshape(b, seq, -1), hg_qig.reshape(b, seq, -1), hgrn_lower_bounds.astype(F32),
                    row(hgrn_out_norm[i]), hg_tri, i)

        h = _merge(h, y_a.reshape(t, -1), y_b.reshape(t, -1), y_c.reshape(t, -1), gates,
                   _layer_bf16(w_br_mla, i), _layer_bf16(w_br_sb, i), _layer_bf16(w_br_hgrn, i), _layer_bf16(w_out, i))
        h = _ffn(h, row(ffn_b_norm[i]), _layer_bf16(ffn_b_w_in, i), _layer_bf16(ffn_b_w_out, i),
                 embed=(p.reshape(depth, t, -1), i, row(ple_norm[i]), _layer_bf16(w_ple_gate, i),
                        _layer_bf16(w_ple_proj, i), row(final_norm)),
                 final=(i == depth - 1))
    return h.reshape(b, seq, d)
```

```python
import functools

import jax
import jax.numpy as jnp
from jax import lax
from jax.experimental import pallas as pl
from jax.experimental.pallas import tpu as pltpu

F32 = jnp.float32
BF16 = jnp.bfloat16

EPS = 1e-6
LB_FLOOR = 1e-30
CHUNK = 64
HG_SUB = 16
MLA_HEADS = 8
MLA_NOPE = 64
MLA_ROPE = 32
MLA_V = 64
MLA_Q_LORA = 384
MLA_KV_LORA = 256
ROPE_BASE = 10000.0
LOG2E = 1.4426950408889634
INV_LN2 = LOG2E
SB_DEAD_LOG2 = 160.0
SB_HEADS = 8
SB_HEAD_DIM = 64
HG_HEADS = 4
HG_KEY = 128
N_BRANCH = 3

LANES = 128
MLA_IN_WIDTH = MLA_Q_LORA + MLA_KV_LORA + 2 * LANES
ATT_BLOCK = 256
ROW_BLOCK = 512
CAST_ROWS = 256
VMEM_LIMIT_BYTES = 56 * 1024 * 1024

NT_DIMS = (((1,), (1,)), ((), ()))
TN_DIMS = (((0,), (0,)), ((), ()))


def _rms(x, w):
    return x * lax.rsqrt(jnp.mean(x * x, axis=-1, keepdims=True) + EPS) * w


def _dot(a, b):
    return jnp.dot(a, b, preferred_element_type=F32)


def _softplus(z):
    return jnp.maximum(z, 0.0) + jnp.log1p(jnp.exp(-jnp.abs(z)))


def _split_bf16(x, terms):
    parts = []
    for _ in range(terms - 1):
        hi = x.astype(BF16)
        parts.append(hi)
        x = x - hi.astype(F32)
    parts.append(x.astype(BF16))
    return parts


def _params(n_axes):
    return pltpu.CompilerParams(dimension_semantics=("arbitrary",) * n_axes,
                                vmem_limit_bytes=VMEM_LIMIT_BYTES)


def _resident(shape):
    return pl.BlockSpec(shape, lambda *_: (0,) * len(shape), pipeline_mode=pl.Buffered(1))


def _cast_kernel(w_ref, o_ref):
    o_ref[...] = w_ref[0].astype(o_ref.dtype)


def _layer_bf16(w, layer):
    _, k, n = w.shape
    tk = CAST_ROWS if k % CAST_ROWS == 0 else k
    return pl.pallas_call(
        _cast_kernel,
        grid=(k // tk,),
        in_specs=[pl.BlockSpec((1, tk, n), lambda r: (layer, r, 0))],
        out_specs=pl.BlockSpec((tk, n), lambda r: (r, 0)),
        out_shape=jax.ShapeDtypeStruct((k, n), BF16),
        compiler_params=_params(1),
        name="to_bf16",
    )(w)


def _ffn_kernel(*refs, d_ff, fc, mix, embed, final):
    refs = list(refs)
    h_ref = refs.pop(0)
    if mix:
        ya_ref, yb_ref, yc_ref, gate_ref, wa_ref, wb_ref, wc_ref, wo_ref = refs[:8]
        del refs[:8]
    nw_ref, win_ref, wout_ref = refs[:3]
    del refs[:3]
    if embed:
        p_ref, pn_ref, wg_ref, wp_ref, fw_ref = refs[:5]
        del refs[:5]
    o_ref, xn_ref, hid_ref = refs

    x = h_ref[...]
    if mix:
        d = x.shape[1]
        merged = None
        for j, (y_ref, w_ref) in enumerate(((ya_ref, wa_ref), (yb_ref, wb_ref), (yc_ref, wc_ref))):
            gate = jax.nn.sigmoid(gate_ref[:, j * d:(j + 1) * d].astype(F32))
            term = gate * _dot(y_ref[...], w_ref[...])
            merged = term if merged is None else merged + term
        x = x + _dot(merged.astype(BF16), wo_ref[...])
    xn_ref[...] = _rms(x, nw_ref[...]).astype(BF16)
    for c in range(d_ff // fc):
        xn = xn_ref[...]
        g = _dot(xn, win_ref[:, c * fc:(c + 1) * fc])
        up = _dot(xn, win_ref[:, d_ff + c * fc:d_ff + (c + 1) * fc])
        hid_ref[:, c * fc:(c + 1) * fc] = (g * jax.nn.sigmoid(g) * up).astype(BF16)
    h = x + 0.5 * _dot(hid_ref[...], wout_ref[...])
    if embed:
        gate = jax.nn.sigmoid(_dot(_rms(h, pn_ref[...]).astype(BF16), wg_ref[...]))
        h = h + _dot(p_ref[0].astype(BF16), wp_ref[...]) * gate
        if final:
            h = _rms(h, fw_ref[...])
    o_ref[...] = h


def _ffn(h, nw, w_in, w_out, mix=None, embed=None, final=False):
    t, d = h.shape
    d_ff = w_out.shape[0]
    tm = min(ROW_BLOCK, t)
    row = lambda n: pl.BlockSpec((tm, n), lambda i: (i, 0))
    in_specs, args = [row(d)], [h]
    if mix is not None:
        in_specs += [row(a.shape[1]) for a in mix[:4]] + [_resident(w.shape) for w in mix[4:]]
        args += list(mix)
    in_specs += [_resident((1, d)), _resident((d, 2 * d_ff)), _resident((d_ff, d))]
    args += [nw, w_in, w_out]
    if embed is not None:
        p, layer, pn, wg, wp, fw = embed
        in_specs += [pl.BlockSpec((1, tm, p.shape[2]), lambda i: (layer, i, 0)),
                     _resident(pn.shape), _resident(wg.shape), _resident(wp.shape), _resident(fw.shape)]
        args += [p, pn, wg, wp, fw]
    return pl.pallas_call(
        functools.partial(_ffn_kernel, d_ff=d_ff, fc=256, mix=mix is not None, embed=embed is not None,
                          final=final),
        grid=(t // tm,),
        in_specs=in_specs,
        out_specs=row(d),
        out_shape=jax.ShapeDtypeStruct((t, d), F32),
        scratch_shapes=[pltpu.VMEM((tm, d), BF16), pltpu.VMEM((tm, d_ff), BF16)],
        compiler_params=_params(1),
        name="ffn",
    )(*args)


def _inproj_kernel(h_ref, nw_ref, w_ref, *o_refs, widths, nc):
    u = _rms(h_ref[...], nw_ref[...]).astype(BF16)
    start = 0
    for o_ref, width in zip(o_refs, widths):
        for c in range(0, width, nc):
            n = min(nc, width - c)
            o_ref[:, c:c + n] = _dot(u, w_ref[:, start + c:start + c + n]).astype(o_ref.dtype)
        start += width


def _inproj(h, nw, w, widths, dtypes):
    t, d = h.shape
    tm = min(ROW_BLOCK, t)
    return pl.pallas_call(
        functools.partial(_inproj_kernel, widths=widths, nc=256),
        grid=(t // tm,),
        in_specs=[pl.BlockSpec((tm, d), lambda i: (i, 0)), _resident((1, d)), _resident(w.shape)],
        out_specs=[pl.BlockSpec((tm, n), lambda i: (i, 0)) for n in widths],
        out_shape=[jax.ShapeDtypeStruct((t, n), dt) for n, dt in zip(widths, dtypes)],
        compiler_params=_params(1),
        name="inproj",
    )(h, nw, w)


def _rope_kernel(pos_ref, inv_ref, cos_ref, sin_ref):
    ang = pos_ref[0].astype(F32) * inv_ref[...]
    cos_ref[0] = jnp.cos(ang)
    sin_ref[0] = jnp.sin(ang)


def _rope_tables(pos, inv):
    b, seq, _ = pos.shape
    out = jax.ShapeDtypeStruct((b, seq, LANES), F32)
    spec = pl.BlockSpec((1, seq, LANES), lambda i: (i, 0, 0))
    return pl.pallas_call(
        _rope_kernel,
        grid=(b,),
        in_specs=[pl.BlockSpec((1, seq, 1), lambda i: (i, 0, 0)), _resident(inv.shape)],
        out_specs=[spec, spec],
        out_shape=[out, out],
        compiler_params=_params(1),
        name="rope_tables",
    )(pos, inv)


def _mla_kernel(x_ref, cos_ref, sin_ref, qn_ref, kvn_ref, wqn_ref, wqr_ref, wqrp_ref, wk_ref, wv_ref,
                o_ref, qcat_ref, kcat_ref, v_ref, *, seq, tq, scale):
    p = pl.program_id(1)
    n_pairs = MLA_HEADS // 2

    @pl.when(p == 0)
    def _project():
        rb = min(ROW_BLOCK, seq)
        for r in range(seq // rb):
            rows = slice(r * rb, (r + 1) * rb)
            x = x_ref[0, rows, :]
            c_q = x[:, :MLA_Q_LORA].astype(F32)
            c_kv = x[:, MLA_Q_LORA:MLA_Q_LORA + MLA_KV_LORA].astype(F32)
            kr = x[:, MLA_Q_LORA + MLA_KV_LORA:MLA_Q_LORA + MLA_KV_LORA + LANES].astype(F32)
            krp = x[:, MLA_Q_LORA + MLA_KV_LORA + LANES:].astype(F32)
            cos, sin = cos_ref[0, rows, :], sin_ref[0, rows, :]
            cqn = _rms(c_q, qn_ref[...]).astype(BF16)
            ckvn = _rms(c_kv, kvn_ref[...]).astype(BF16)
            k_rope = (kr * cos + krp * sin).astype(BF16)
            halves = lambda a: (a[:, :LANES], a[:, LANES:])
            ropes = zip(halves(_dot(cqn, wqr_ref[...])), halves(_dot(cqn, wqrp_ref[...])))
            for g, (qr, qrp) in enumerate(ropes):
                q_rope = ((qr * cos + qrp * sin) * scale).astype(BF16)
                qcat_ref[2 * g, rows, LANES:] = q_rope
                qcat_ref[2 * g + 1, rows, LANES:] = q_rope
            for g in range(n_pairs // 2):
                cols = slice(2 * g * LANES, (2 * g + 2) * LANES)
                parts = zip(halves(_dot(cqn, wqn_ref[:, cols])), halves(_dot(ckvn, wk_ref[:, cols])),
                            halves(_dot(ckvn, wv_ref[:, cols])))
                for pp, (qn, kn, vv) in enumerate(parts, start=2 * g):
                    qcat_ref[pp, rows, :LANES] = (qn * scale).astype(BF16)
                    kcat_ref[pp, rows, :LANES] = kn.astype(BF16)
                    kcat_ref[pp, rows, LANES:] = k_rope
                    v_ref[pp, rows, :] = vv.astype(BF16)

    lane = lax.broadcasted_iota(jnp.int32, (1, 2 * LANES), 1)
    out_lane = lax.broadcasted_iota(jnp.int32, (1, LANES), 1)
    ri = lax.broadcasted_iota(jnp.int32, (tq, tq), 0)
    ci = lax.broadcasted_iota(jnp.int32, (tq, tq), 1)
    diag_mask = (ci // CHUNK) <= (ri // CHUNK)
    group = jnp.where(lane < LANES, lane // MLA_NOPE, 2 + (lane - LANES) // MLA_ROPE)

    for n in range(seq // tq):
        keys = [slice(j * tq, (j + 1) * tq) for j in range(n + 1)]
        qc = qcat_ref[p, keys[n], :]
        s = []
        for hh in range(2):
            own = (group == hh) | (group == 2 + (p % 2) * 2 + hh)
            qh = qc * jnp.where(own, 1.0, 0.0).astype(BF16)
            sh = [lax.dot_general(qh, kcat_ref[p, kj, :], NT_DIMS, preferred_element_type=F32)
                  for kj in keys]
            sh[n] = jnp.where(diag_mask, sh[n], -jnp.inf)
            s.append(sh)
        m = [jnp.max(functools.reduce(jnp.maximum, sh), axis=-1, keepdims=True) for sh in s]
        outs = []
        for hh in range(2):
            pr = [jnp.exp2(sj - m[hh]) for sj in s[hh]]
            l = jnp.sum(functools.reduce(jnp.add, pr), axis=-1, keepdims=True)
            acc = None
            for j, kj in enumerate(keys):
                t = _dot(pr[j].astype(BF16), v_ref[p, kj, :])
                acc = t if acc is None else acc + t
            outs.append(acc * (1.0 / l))
        o_ref[0, keys[n], :] = jnp.where(out_lane < MLA_V, outs[0], outs[1]).astype(o_ref.dtype)


def _mla(x, cos, sin, qn, kvn, wqn, wqr, wqrp, wk, wv):
    b, seq, _ = x.shape
    n_pairs = MLA_HEADS // 2
    tq = min(ATT_BLOCK, seq)
    scale = float((MLA_NOPE + MLA_ROPE) ** -0.5) * LOG2E
    return pl.pallas_call(
        functools.partial(_mla_kernel, seq=seq, tq=tq, scale=scale),
        grid=(b, n_pairs),
        in_specs=[pl.BlockSpec((1, seq, MLA_IN_WIDTH), lambda i, j: (i, 0, 0)),
                  pl.BlockSpec((1, seq, LANES), lambda i, j: (i, 0, 0)),
                  pl.BlockSpec((1, seq, LANES), lambda i, j: (i, 0, 0)),
                  _resident(qn.shape), _resident(kvn.shape),
                  _resident(wqn.shape), _resident(wqr.shape), _resident(wqrp.shape),
                  _resident(wk.shape), _resident(wv.shape)],
        out_specs=pl.BlockSpec((1, seq, LANES), lambda i, j: (i, 0, j)),
        out_shape=jax.ShapeDtypeStruct((b, seq, MLA_HEADS * MLA_V), BF16),
        scratch_shapes=[pltpu.VMEM((n_pairs, seq, 2 * LANES), BF16),
                        pltpu.VMEM((n_pairs, seq, 2 * LANES), BF16),
                        pltpu.VMEM((n_pairs, seq, LANES), BF16)],
        compiler_params=_params(2),
        name="mla",
    )(x, cos, sin, qn, kvn, wqn, wqr, wqrp, wk, wv)


def _sb_kernel(q_ref, k_ref, v_ref, tri_ref, o_ref, *, tq, n_q):
    lane = lax.broadcasted_iota(jnp.int32, (1, LANES), 1)
    ri = lax.broadcasted_iota(jnp.int32, (tq, tq), 0)
    ci = lax.broadcasted_iota(jnp.int32, (tq, tq), 1)
    diag_mask = ci < ri
    sign = jnp.int32(-2 ** 31)
    rows = [slice(j * tq, (j + 1) * tq) for j in range(n_q)]

    def sweep(blocks, state):
        tiles = [(n, hh, j) for n, js in blocks.items() for j in js for hh in range(2)]
        qhs = {}
        for n in blocks:
            q = q_ref[0, rows[n], :]
            for hh in range(2):
                qhs[n, hh] = jnp.where((lane // SB_HEAD_DIM) == hh, q, jnp.zeros_like(q))
        z2 = {t: lax.dot_general(qhs[t[0], t[1]], k_ref[0, rows[t[2]], :], NT_DIMS,
                                 preferred_element_type=F32) for t in tiles}
        suffix, total = {}, {}
        for t in tiles:
            neg_abs = lax.bitcast_convert_type(lax.bitcast_convert_type(z2[t], jnp.int32) | sign, F32)
            sp2 = jnp.maximum(z2[t], 0.0) + jnp.log(1.0 + jnp.exp2(neg_abs)) * INV_LN2
            keep = jnp.where(diag_mask, sp2, 0.0) if t[2] == t[0] else sp2
            suffix[t] = _dot(keep.astype(BF16), tri_ref[...])
            total[t] = jnp.sum(keep, axis=-1, keepdims=True)
        state = dict(state)
        for n, hh, j in tiles:
            rest, acc = state[n, hh]
            a = jnp.exp2(jnp.minimum(z2[n, hh, j] - suffix[n, hh, j] - rest, 0.0))
            if j == n:
                a = jnp.where(diag_mask, a, 0.0)
            pv = _dot(a.astype(BF16), v_ref[0, rows[j], :])
            state[n, hh] = (rest + total[n, hh, j], pv if acc is None else acc + pv)
        return state

    def store(n, state):
        o_ref[0, rows[n], :] = jnp.where(lane < SB_HEAD_DIM, state[n, 0][1], state[n, 1][1]).astype(o_ref.dtype)

    start = {(n, hh): (jnp.zeros((tq, 1), F32), None) for n in range(n_q) for hh in range(2)}
    near = sweep({n: [j for j in (n, n - 1) if j >= 0] for n in range(n_q)}, start)
    for n in range(n_q):
        store(n, near)

    for n in range(2, n_q):
        alive = jnp.minimum(jnp.min(near[n, 0][0]), jnp.min(near[n, 1][0])) < SB_DEAD_LOG2

        @pl.when(alive)
        def _(n=n):
            far = sweep({n: list(range(n - 2, -1, -1))}, {k: v for k, v in near.items() if k[0] == n})
            store(n, far)


def _sb(qkv, tri):
    b, seq, _ = qkv.shape
    n_pairs = SB_HEADS // 2
    tq = min(ATT_BLOCK, seq)
    col = lambda off: pl.BlockSpec((1, seq, LANES), lambda i, j: (i, 0, off + j))
    return pl.pallas_call(
        functools.partial(_sb_kernel, tq=tq, n_q=seq // tq),
        grid=(b, n_pairs),
        in_specs=[col(0), col(n_pairs), col(2 * n_pairs), _resident(tri.shape)],
        out_specs=col(0),
        out_shape=jax.ShapeDtypeStruct((b, seq, SB_HEADS * SB_HEAD_DIM), BF16),
        compiler_params=_params(2),
        name="stickbreak",
    )(qkv, qkv, qkv, tri)


def _hgrn_kernel(f_ref, qig_ref, lbp_ref, nw_ref, tri_ref, o_ref, b_ref, *, seq, layer, unroll):
    width = HG_HEADS * HG_KEY
    lbp = lbp_ref[...]
    e = jnp.exp(lbp - jnp.max(lbp, axis=0, keepdims=True))
    sm = e / jnp.sum(e, axis=0, keepdims=True)
    lb = jnp.zeros((1, width), F32)
    for j in range(1, layer + 1):
        lb = lb + sm[j:j + 1, :]
    lb = jnp.clip(lb, 0.0, 1.0 - 1e-6)
    lb_floor = jnp.maximum(lb, LB_FLOOR)

    gb = tri_ref.shape[0]
    for r in range(seq // gb):
        rows = slice(r * gb, (r + 1) * gb)
        log_f = jnp.minimum(jnp.log(lb_floor + (1.0 - lb) * jax.nn.sigmoid(f_ref[0, rows, :])), 0.0)
        acc = None
        for part in _split_bf16(log_f, 2):
            t = _dot(tri_ref[...], part)
            acc = t if acc is None else acc + t
        b_ref[rows, :] = acc

    n_sub = CHUNK // HG_SUB
    causal = (lax.broadcasted_iota(jnp.int32, (CHUNK, CHUNK), 1)
              <= lax.broadcasted_iota(jnp.int32, (CHUNK, CHUNK), 0))

    def step(n, states):
        states = list(states)
        tiles = [(u, h) for u in range(unroll) for h in range(HG_HEADS)]
        rows = {u: pl.ds(pl.multiple_of((n * unroll + u) * CHUNK, CHUNK), CHUNK) for u in range(unroll)}
        cols = {h: slice(h * HG_KEY, (h + 1) * HG_KEY) for h in range(HG_HEADS)}

        work = {}
        for u, h in tiles:
            x = f_ref[0, rows[u], cols[h]]
            b = b_ref[rows[u], cols[h]]
            q_raw = qig_ref[0, rows[u], cols[h]].astype(F32)
            q = q_raw * jax.nn.sigmoid(q_raw)
            k = (1.0 - lb[:, cols[h]]) * jax.nn.sigmoid(-x)
            v16 = qig_ref[0, rows[u], width + h * HG_KEY:width + (h + 1) * HG_KEY]
            b_last = b[CHUNK - 1:CHUNK, :]
            qd = (q * jnp.exp(b)).astype(BF16)
            kd = (k * jnp.exp(b_last - b)).astype(BF16)
            update = lax.dot_general(v16, kd, TN_DIMS, preferred_element_type=F32)
            qf, kf = [], []
            for i in range(n_sub):
                lo, hi = i * HG_SUB, (i + 1) * HG_SUB
                b_ref_i = jnp.zeros((1, HG_KEY), F32) if i == 0 else b[lo - 1:lo, :]
                qf_i = q[lo:hi, :] * jnp.exp(b[lo:hi, :] - b_ref_i)
                above = [jnp.zeros((lo, HG_KEY), F32)] if lo else []
                below = [jnp.zeros((CHUNK - hi, HG_KEY), F32)] if hi < CHUNK else []
                qf.append(jnp.concatenate(above + [qf_i] + below, axis=0))
                kf.append(jnp.concatenate([k[:hi, :] * jnp.exp(b_ref_i - b[:hi, :])] + below, axis=0))
            a = lax.dot_general(jnp.concatenate(qf, axis=1).astype(BF16),
                                jnp.concatenate(kf, axis=1).astype(BF16), NT_DIMS,
                                preferred_element_type=F32)
            a = jnp.where(causal, a, 0.0).astype(BF16)
            work[u, h] = (qd, update, jnp.exp(b_last), a, v16)

        inter = {}
        for u, h in tiles:
            qd, update, decay, _, _ = work[u, h]
            inter[u, h] = lax.dot_general(qd, states[h].astype(BF16), NT_DIMS, preferred_element_type=F32)
            states[h] = states[h] * decay + update

        for u, h in tiles:
            _, _, _, a, v16 = work[u, h]
            o = inter[u, h] + _dot(a, v16)
            o = o * lax.rsqrt(jnp.mean(o * o, axis=-1, keepdims=True) + EPS) * nw_ref[:, cols[h]]
            g = qig_ref[0, rows[u], 2 * width + h * HG_KEY:2 * width + (h + 1) * HG_KEY].astype(F32)
            o_ref[0, rows[u], cols[h]] = (o * (g * jax.nn.sigmoid(g))).astype(o_ref.dtype)
        return tuple(states)

    zero = jnp.zeros((HG_KEY, HG_KEY), F32)
    lax.fori_loop(0, seq // (CHUNK * unroll), step, (zero,) * HG_HEADS)


def _hgrn(hg_f, hg_qig, lbp, nw, tri, layer):
    b, seq, width = hg_f.shape
    return pl.pallas_call(
        functools.partial(_hgrn_kernel, seq=seq, layer=layer, unroll=4),
        grid=(b,),
        in_specs=[pl.BlockSpec((1, seq, width), lambda i: (i, 0, 0)),
                  pl.BlockSpec((1, seq, 3 * width), lambda i: (i, 0, 0)),
                  _resident(lbp.shape), _resident(nw.shape), _resident(tri.shape)],
        out_specs=pl.BlockSpec((1, seq, width), lambda i: (i, 0, 0)),
        out_shape=jax.ShapeDtypeStruct((b, seq, width), BF16),
        scratch_shapes=[pltpu.VMEM((seq, width), F32)],
        compiler_params=_params(1),
        name="hgrn2",
    )(hg_f, hg_qig, lbp, nw, tri)


def _rot_half(w):
    half = MLA_ROPE // 2
    return jnp.concatenate([-w[..., half:], w[..., :half]], axis=-1)


def _lower(n):
    r = lax.broadcasted_iota(jnp.int32, (n, n), 0)
    c = lax.broadcasted_iota(jnp.int32, (n, n), 1)
    return r >= c


def kernel(x, p, positions, ffn_a_norm, ffn_a_w_in, ffn_a_w_out, mix_norm, w_in, mla_q_norm, mla_w_uq, mla_kv_norm, mla_w_ukv, hgrn_lower_bounds, hgrn_out_norm, w_br_mla, w_br_sb, w_br_hgrn, w_out, ffn_b_norm, ffn_b_w_in, ffn_b_w_out, ple_norm, w_ple_gate, w_ple_proj, final_norm):
    b, seq, d = x.shape
    depth = ffn_a_norm.shape[0]
    t = b * seq
    bf = lambda a: a.astype(BF16)
    row = lambda a: a.reshape(1, -1).astype(F32)

    tq = min(ATT_BLOCK, seq)
    sb_tri = _lower(tq).astype(BF16)
    gb = 4 * CHUNK
    r = lax.broadcasted_iota(jnp.int32, (gb, gb), 0)
    c = lax.broadcasted_iota(jnp.int32, (gb, gb), 1)
    hg_tri = ((c <= r) & (c // CHUNK == r // CHUNK)).astype(BF16)

    half = MLA_ROPE // 2
    inv = ROPE_BASE ** (-jnp.arange(half, dtype=F32) / half)
    inv = jnp.tile(inv, LANES // half).reshape(1, LANES)
    cos, sin = _rope_tables(positions.reshape(b, seq, 1), inv)

    splits = (MLA_Q_LORA, MLA_KV_LORA, MLA_ROPE, 3 * SB_HEADS * SB_HEAD_DIM,
              4 * HG_HEADS * HG_KEY, N_BRANCH * d)
    hg_w = HG_HEADS * HG_KEY
    widths = (MLA_IN_WIDTH, splits[3], hg_w, 3 * hg_w, splits[5])

    h = x.reshape(t, d)
    for i in range(depth):
        h = _ffn(h, row(ffn_a_norm[i]), _layer_bf16(ffn_a_w_in, i), _layer_bf16(ffn_a_w_out, i))

        w = w_in[i]
        offs = [0]
        for n in splits:
            offs.append(offs[-1] + n)
        w_cq, w_ckv, w_kr, w_sb, w_hg, w_gate = (w[:, offs[j]:offs[j + 1]] for j in range(6))
        n_sbq = SB_HEADS * SB_HEAD_DIM
        w_sb = jnp.concatenate([w_sb[:, :n_sbq] * (SB_HEAD_DIM ** -0.5 * LOG2E), w_sb[:, n_sbq:]], axis=1)
        w_hq, w_hf, w_hi, w_hgate = (w_hg[:, j * hg_w:(j + 1) * hg_w] for j in range(4))
        w_cat = bf(jnp.concatenate([w_cq, w_ckv, jnp.tile(w_kr, (1, 4)), jnp.tile(_rot_half(w_kr), (1, 4)),
                                    w_sb, w_hf, w_hq, w_hi, w_hgate, w_gate], axis=1))
        mla_in, sb_in, hg_f, hg_qig, gates = _inproj(h, row(mix_norm[i]), w_cat, widths,
                                                     (BF16, BF16, F32, BF16, BF16))

        wq = mla_w_uq[i].reshape(MLA_Q_LORA, MLA_HEADS, MLA_NOPE + MLA_ROPE)
        wq_nope = bf(wq[:, :, :MLA_NOPE].reshape(MLA_Q_LORA, -1))
        wq_rope = wq[:, :, MLA_NOPE:]
        wkv = mla_w_ukv[i].reshape(MLA_KV_LORA, MLA_HEADS, MLA_NOPE + MLA_V)
        y_a = _mla(mla_in.reshape(b, seq, -1), cos, sin, row(mla_q_norm[i]), row(mla_kv_norm[i]),
                   wq_nope, bf(wq_rope.reshape(MLA_Q_LORA, -1)), bf(_rot_half(wq_rope).reshape(MLA_Q_LORA, -1)),
                   bf(wkv[:, :, :MLA_NOPE].reshape(MLA_KV_LORA, -1)), bf(wkv[:, :, MLA_NOPE:].reshape(MLA_KV_LORA, -1)))
        y_b = _sb(sb_in.reshape(b, seq, -1), sb_tri)
        y_c = _hgrn(hg_f.reshape(b, seq, -1), hg_qig.reshape(b, seq, -1), hgrn_lower_bounds.astype(F32),
                    row(hgrn_out_norm[i]), hg_tri, i)

        h = _ffn(h, row(ffn_b_norm[i]), _layer_bf16(ffn_b_w_in, i), _layer_bf16(ffn_b_w_out, i),
                 mix=(y_a.reshape(t, -1), y_b.reshape(t, -1), y_c.reshape(t, -1), gates, _layer_bf16(w_br_mla, i),
                      _layer_bf16(w_br_sb, i), _layer_bf16(w_br_hgrn, i), _layer_bf16(w_out, i)),
                 embed=(p.reshape(depth, t, -1), i, row(ple_norm[i]), _layer_bf16(w_ple_gate, i),
                        _layer_bf16(w_ple_proj, i), row(final_norm)),
                 final=(i == depth - 1))
    return h.reshape(b, seq, d)
```

```python
import functools

import jax
import jax.numpy as jnp
from jax import lax
from jax.experimental import pallas as pl
from jax.experimental.pallas import tpu as pltpu

F32 = jnp.float32
BF16 = jnp.bfloat16

EPS = 1e-6
LB_FLOOR = 1e-30
CHUNK = 64
HG_SUB = 16
MLA_HEADS = 8
MLA_NOPE = 64
MLA_ROPE = 32
MLA_V = 64
MLA_Q_LORA = 384
MLA_KV_LORA = 256
ROPE_BASE = 10000.0
LOG2E = 1.4426950408889634
INV_LN2 = LOG2E
SB_DEAD_LOG2 = 160.0
SB_HEADS = 8
SB_HEAD_DIM = 64
HG_HEADS = 4
HG_KEY = 128
N_BRANCH = 3

LANES = 128
MLA_IN_WIDTH = MLA_Q_LORA + MLA_KV_LORA + 2 * LANES
ATT_BLOCK = 256
ROW_BLOCK = 512
CAST_ROWS = 256
VMEM_LIMIT_BYTES = 56 * 1024 * 1024

NT_DIMS = (((1,), (1,)), ((), ()))
TN_DIMS = (((0,), (0,)), ((), ()))


def _rms(x, w):
    return x * lax.rsqrt(jnp.mean(x * x, axis=-1, keepdims=True) + EPS) * w


def _dot(a, b):
    return jnp.dot(a, b, preferred_element_type=F32)


def _softplus(z):
    return jnp.maximum(z, 0.0) + jnp.log1p(jnp.exp(-jnp.abs(z)))


def _split_bf16(x, terms):
    parts = []
    for _ in range(terms - 1):
        hi = x.astype(BF16)
        parts.append(hi)
        x = x - hi.astype(F32)
    parts.append(x.astype(BF16))
    return parts


def _params(n_axes):
    return pltpu.CompilerParams(dimension_semantics=("arbitrary",) * n_axes,
                                vmem_limit_bytes=VMEM_LIMIT_BYTES)


def _resident(shape):
    return pl.BlockSpec(shape, lambda *_: (0,) * len(shape), pipeline_mode=pl.Buffered(1))


def _cast_kernel(w_ref, o_ref):
    o_ref[...] = w_ref[0].astype(o_ref.dtype)


def _layer_bf16(w, layer):
    _, k, n = w.shape
    tk = CAST_ROWS if k % CAST_ROWS == 0 else k
    return pl.pallas_call(
        _cast_kernel,
        grid=(k // tk,),
        in_specs=[pl.BlockSpec((1, tk, n), lambda r: (layer, r, 0))],
        out_specs=pl.BlockSpec((tk, n), lambda r: (r, 0)),
        out_shape=jax.ShapeDtypeStruct((k, n), BF16),
        compiler_params=_params(1),
        name="to_bf16",
    )(w)


def _ffn_kernel(*refs, d_ff, fc, mix, embed, final):
    refs = list(refs)
    h_ref = refs.pop(0)
    if mix:
        ya_ref, yb_ref, yc_ref, gate_ref, wa_ref, wb_ref, wc_ref, wo_ref = refs[:8]
        del refs[:8]
    nw_ref, win_ref, wout_ref = refs[:3]
    del refs[:3]
    if embed:
        p_ref, pn_ref, wg_ref, wp_ref, fw_ref = refs[:5]
        del refs[:5]
    o_ref, xn_ref, hid_ref = refs

    x = h_ref[...]
    if mix:
        d = x.shape[1]
        merged = None
        for j, (y_ref, w_ref) in enumerate(((ya_ref, wa_ref), (yb_ref, wb_ref), (yc_ref, wc_ref))):
            gate = jax.nn.sigmoid(gate_ref[:, j * d:(j + 1) * d].astype(F32))
            term = gate * _dot(y_ref[...], w_ref[...])
            merged = term if merged is None else merged + term
        x = x + _dot(merged.astype(BF16), wo_ref[...])
    xn_ref[...] = _rms(x, nw_ref[...]).astype(BF16)
    for c in range(d_ff // fc):
        xn = xn_ref[...]
        g = _dot(xn, win_ref[:, c * fc:(c + 1) * fc])
        up = _dot(xn, win_ref[:, d_ff + c * fc:d_ff + (c + 1) * fc])
        hid_ref[:, c * fc:(c + 1) * fc] = (g * jax.nn.sigmoid(g) * up).astype(BF16)
    h = x + 0.5 * _dot(hid_ref[...], wout_ref[...])
    if embed:
        gate = jax.nn.sigmoid(_dot(_rms(h, pn_ref[...]).astype(BF16), wg_ref[...]))
        h = h + _dot(p_ref[0].astype(BF16), wp_ref[...]) * gate
        if final:
            h = _rms(h, fw_ref[...])
    o_ref[...] = h


def _ffn(h, nw, w_in, w_out, mix=None, embed=None, final=False):
    t, d = h.shape
    d_ff = w_out.shape[0]
    tm = min(ROW_BLOCK, t)
    row = lambda n: pl.BlockSpec((tm, n), lambda i: (i, 0))
    in_specs, args = [row(d)], [h]
    if mix is not None:
        in_specs += [row(a.shape[1]) for a in mix[:4]] + [_resident(w.shape) for w in mix[4:]]
        args += list(mix)
    in_specs += [_resident((1, d)), _resident((d, 2 * d_ff)), _resident((d_ff, d))]
    args += [nw, w_in, w_out]
    if embed is not None:
        p, layer, pn, wg, wp, fw = embed
        in_specs += [pl.BlockSpec((1, tm, p.shape[2]), lambda i: (layer, i, 0)),
                     _resident(pn.shape), _resident(wg.shape), _resident(wp.shape), _resident(fw.shape)]
        args += [p, pn, wg, wp, fw]
    return pl.pallas_call(
        functools.partial(_ffn_kernel, d_ff=d_ff, fc=256, mix=mix is not None, embed=embed is not None,
                          final=final),
        grid=(t // tm,),
        in_specs=in_specs,
        out_specs=row(d),
        out_shape=jax.ShapeDtypeStruct((t, d), F32),
        scratch_shapes=[pltpu.VMEM((tm, d), BF16), pltpu.VMEM((tm, d_ff), BF16)],
        compiler_params=_params(1),
        name="ffn",
    )(*args)


def _inproj_kernel(h_ref, nw_ref, w_ref, *o_refs, widths, nc):
    u = _rms(h_ref[...], nw_ref[...]).astype(BF16)
    start = 0
    for o_ref, width in zip(o_refs, widths):
        for c in range(0, width, nc):
            n = min(nc, width - c)
            o_ref[:, c:c + n] = _dot(u, w_ref[:, start + c:start + c + n]).astype(o_ref.dtype)
        start += width


def _inproj(h, nw, w, widths, dtypes):
    t, d = h.shape
    tm = min(ROW_BLOCK, t)
    return pl.pallas_call(
        functools.partial(_inproj_kernel, widths=widths, nc=256),
        grid=(t // tm,),
        in_specs=[pl.BlockSpec((tm, d), lambda i: (i, 0)), _resident((1, d)), _resident(w.shape)],
        out_specs=[pl.BlockSpec((tm, n), lambda i: (i, 0)) for n in widths],
        out_shape=[jax.ShapeDtypeStruct((t, n), dt) for n, dt in zip(widths, dtypes)],
        compiler_params=_params(1),
        name="inproj",
    )(h, nw, w)


def _rope_kernel(pos_ref, inv_ref, cos_ref, sin_ref):
    ang = pos_ref[0].astype(F32) * inv_ref[...]
    cos_ref[0] = jnp.cos(ang)
    sin_ref[0] = jnp.sin(ang)


def _rope_tables(pos, inv):
    b, seq, _ = pos.shape
    out = jax.ShapeDtypeStruct((b, seq, LANES), F32)
    spec = pl.BlockSpec((1, seq, LANES), lambda i: (i, 0, 0))
    return pl.pallas_call(
        _rope_kernel,
        grid=(b,),
        in_specs=[pl.BlockSpec((1, seq, 1), lambda i: (i, 0, 0)), _resident(inv.shape)],
        out_specs=[spec, spec],
        out_shape=[out, out],
        compiler_params=_params(1),
        name="rope_tables",
    )(pos, inv)


def _mla_kernel(x_ref, cos_ref, sin_ref, qn_ref, kvn_ref, wqn_ref, wqr_ref, wqrp_ref, wk_ref, wv_ref,
                o_ref, qcat_ref, kcat_ref, v_ref, *, seq, tq, scale):
    p = pl.program_id(1)
    n_pairs = MLA_HEADS // 2

    @pl.when(p == 0)
    def _project():
        rb = min(ROW_BLOCK, seq)
        for r in range(seq // rb):
            rows = slice(r * rb, (r + 1) * rb)
            x = x_ref[0, rows, :]
            c_q = x[:, :MLA_Q_LORA].astype(F32)
            c_kv = x[:, MLA_Q_LORA:MLA_Q_LORA + MLA_KV_LORA].astype(F32)
            kr = x[:, MLA_Q_LORA + MLA_KV_LORA:MLA_Q_LORA + MLA_KV_LORA + LANES].astype(F32)
            krp = x[:, MLA_Q_LORA + MLA_KV_LORA + LANES:].astype(F32)
            cos, sin = cos_ref[0, rows, :], sin_ref[0, rows, :]
            cqn = _rms(c_q, qn_ref[...]).astype(BF16)
            ckvn = _rms(c_kv, kvn_ref[...]).astype(BF16)
            k_rope = (kr * cos + krp * sin).astype(BF16)
            halves = lambda a: (a[:, :LANES], a[:, LANES:])
            ropes = zip(halves(_dot(cqn, wqr_ref[...])), halves(_dot(cqn, wqrp_ref[...])))
            for g, (qr, qrp) in enumerate(ropes):
                q_rope = ((qr * cos + qrp * sin) * scale).astype(BF16)
                qcat_ref[2 * g, rows, LANES:] = q_rope
                qcat_ref[2 * g + 1, rows, LANES:] = q_rope
            for g in range(n_pairs // 2):
                cols = slice(2 * g * LANES, (2 * g + 2) * LANES)
                parts = zip(halves(_dot(cqn, wqn_ref[:, cols])), halves(_dot(ckvn, wk_ref[:, cols])),
                            halves(_dot(ckvn, wv_ref[:, cols])))
                for pp, (qn, kn, vv) in enumerate(parts, start=2 * g):
                    qcat_ref[pp, rows, :LANES] = (qn * scale).astype(BF16)
                    kcat_ref[pp, rows, :LANES] = kn.astype(BF16)
                    kcat_ref[pp, rows, LANES:] = k_rope
                    v_ref[pp, rows, :] = vv.astype(BF16)

    lane = lax.broadcasted_iota(jnp.int32, (1, 2 * LANES), 1)
    out_lane = lax.broadcasted_iota(jnp.int32, (1, LANES), 1)
    ri = lax.broadcasted_iota(jnp.int32, (tq, tq), 0)
    ci = lax.broadcasted_iota(jnp.int32, (tq, tq), 1)
    diag_mask = (ci // CHUNK) <= (ri // CHUNK)
    group = jnp.where(lane < LANES, lane // MLA_NOPE, 2 + (lane - LANES) // MLA_ROPE)

    for n in range(seq // tq):
        keys = [slice(j * tq, (j + 1) * tq) for j in range(n + 1)]
        qc = qcat_ref[p, keys[n], :]
        s = []
        for hh in range(2):
            own = (group == hh) | (group == 2 + (p % 2) * 2 + hh)
            qh = qc * jnp.where(own, 1.0, 0.0).astype(BF16)
            sh = [lax.dot_general(qh, kcat_ref[p, kj, :], NT_DIMS, preferred_element_type=F32)
                  for kj in keys]
            sh[n] = jnp.where(diag_mask, sh[n], -jnp.inf)
            s.append(sh)
        m = [jnp.max(functools.reduce(jnp.maximum, sh), axis=-1, keepdims=True) for sh in s]
        outs = []
        for hh in range(2):
            pr = [jnp.exp2(sj - m[hh]) for sj in s[hh]]
            l = jnp.sum(functools.reduce(jnp.add, pr), axis=-1, keepdims=True)
            acc = None
            for j, kj in enumerate(keys):
                t = _dot(pr[j].astype(BF16), v_ref[p, kj, :])
                acc = t if acc is None else acc + t
            outs.append(acc * (1.0 / l))
        o_ref[0, keys[n], :] = jnp.where(out_lane < MLA_V, outs[0], outs[1]).astype(o_ref.dtype)


def _sb_kernel(q_ref, k_ref, v_ref, tri_ref, o_ref, *, tq, n_q):
    lane = lax.broadcasted_iota(jnp.int32, (1, LANES), 1)
    ri = lax.broadcasted_iota(jnp.int32, (tq, tq), 0)
    ci = lax.broadcasted_iota(jnp.int32, (tq, tq), 1)
    diag_mask = ci < ri
    sign = jnp.int32(-2 ** 31)
    rows = [slice(j * tq, (j + 1) * tq) for j in range(n_q)]

    def sweep(blocks, state):
        tiles = [(n, hh, j) for n, js in blocks.items() for j in js for hh in range(2)]
        qhs = {}
        for n in blocks:
            q = q_ref[0, rows[n], :]
            for hh in range(2):
                qhs[n, hh] = jnp.where((lane // SB_HEAD_DIM) == hh, q, jnp.zeros_like(q))
        z2 = {t: lax.dot_general(qhs[t[0], t[1]], k_ref[0, rows[t[2]], :], NT_DIMS,
                                 preferred_element_type=F32) for t in tiles}
        suffix, total = {}, {}
        for t in tiles:
            neg_abs = lax.bitcast_convert_type(lax.bitcast_convert_type(z2[t], jnp.int32) | sign, F32)
            sp2 = jnp.maximum(z2[t], 0.0) + jnp.log(1.0 + jnp.exp2(neg_abs)) * INV_LN2
            keep = jnp.where(diag_mask, sp2, 0.0) if t[2] == t[0] else sp2
            suffix[t] = _dot(keep.astype(BF16), tri_ref[...])
            total[t] = jnp.sum(keep, axis=-1, keepdims=True)
        state = dict(state)
        for n, hh, j in tiles:
            rest, acc = state[n, hh]
            a = jnp.exp2(jnp.minimum(z2[n, hh, j] - suffix[n, hh, j] - rest, 0.0))
            if j == n:
                a = jnp.where(diag_mask, a, 0.0)
            pv = _dot(a.astype(BF16), v_ref[0, rows[j], :])
            state[n, hh] = (rest + total[n, hh, j], pv if acc is None else acc + pv)
        return state

    def store(n, state):
        o_ref[0, rows[n], :] = jnp.where(lane < SB_HEAD_DIM, state[n, 0][1], state[n, 1][1]).astype(o_ref.dtype)

    start = {(n, hh): (jnp.zeros((tq, 1), F32), None) for n in range(n_q) for hh in range(2)}
    near = sweep({n: [j for j in (n, n - 1) if j >= 0] for n in range(n_q)}, start)
    for n in range(n_q):
        store(n, near)

    for n in range(2, n_q):
        alive = jnp.minimum(jnp.min(near[n, 0][0]), jnp.min(near[n, 1][0])) < SB_DEAD_LOG2

        @pl.when(alive)
        def _(n=n):
            far = sweep({n: list(range(n - 2, -1, -1))}, {k: v for k, v in near.items() if k[0] == n})
            store(n, far)


def _mla_sb_kernel(x_ref, cos_ref, sin_ref, qn_ref, kvn_ref, wqn_ref, wqr_ref, wqrp_ref, wk_ref, wv_ref,
                   sq_ref, sk_ref, sv_ref, tri_ref, oa_ref, ob_ref, qcat_ref, kcat_ref, v_ref, *, seq, tq, scale):
    _mla_kernel(x_ref, cos_ref, sin_ref, qn_ref, kvn_ref, wqn_ref, wqr_ref, wqrp_ref, wk_ref, wv_ref,
                oa_ref, qcat_ref, kcat_ref, v_ref, seq=seq, tq=tq, scale=scale)
    _sb_kernel(sq_ref, sk_ref, sv_ref, tri_ref, ob_ref, tq=tq, n_q=seq // tq)


def _mla_sb(x, cos, sin, qn, kvn, wqn, wqr, wqrp, wk, wv, qkv, tri):
    b, seq, _ = x.shape
    assert MLA_HEADS == SB_HEADS and MLA_V == SB_HEAD_DIM
    n_pairs = MLA_HEADS // 2
    tq = min(ATT_BLOCK, seq)
    scale = float((MLA_NOPE + MLA_ROPE) ** -0.5) * LOG2E
    col = lambda off: pl.BlockSpec((1, seq, LANES), lambda i, j: (i, 0, off + j))
    full = lambda n: pl.BlockSpec((1, seq, n), lambda i, j: (i, 0, 0))
    out = jax.ShapeDtypeStruct((b, seq, MLA_HEADS * MLA_V), BF16)
    return pl.pallas_call(
        functools.partial(_mla_sb_kernel, seq=seq, tq=tq, scale=scale),
        grid=(b, n_pairs),
        in_specs=[full(MLA_IN_WIDTH), full(LANES), full(LANES),
                  _resident(qn.shape), _resident(kvn.shape),
                  _resident(wqn.shape), _resident(wqr.shape), _resident(wqrp.shape),
                  _resident(wk.shape), _resident(wv.shape),
                  col(0), col(n_pairs), col(2 * n_pairs), _resident(tri.shape)],
        out_specs=[col(0), col(0)],
        out_shape=[out, out],
        scratch_shapes=[pltpu.VMEM((n_pairs, seq, 2 * LANES), BF16),
                        pltpu.VMEM((n_pairs, seq, 2 * LANES), BF16),
                        pltpu.VMEM((n_pairs, seq, LANES), BF16)],
        compiler_params=_params(2),
        name="mla_sb",
    )(x, cos, sin, qn, kvn, wqn, wqr, wqrp, wk, wv, qkv, qkv, qkv, tri)


def _hgrn_kernel(f_ref, qig_ref, lbp_ref, nw_ref, tri_ref, o_ref, b_ref, *, seq, layer, unroll):
    width = HG_HEADS * HG_KEY
    lbp = lbp_ref[...]
    e = jnp.exp(lbp - jnp.max(lbp, axis=0, keepdims=True))
    sm = e / jnp.sum(e, axis=0, keepdims=True)
    lb = jnp.zeros((1, width), F32)
    for j in range(1, layer + 1):
        lb = lb + sm[j:j + 1, :]
    lb = jnp.clip(lb, 0.0, 1.0 - 1e-6)
    lb_floor = jnp.maximum(lb, LB_FLOOR)

    gb = tri_ref.shape[0]
    for r in range(seq // gb):
        rows = slice(r * gb, (r + 1) * gb)
        log_f = jnp.minimum(jnp.log(lb_floor + (1.0 - lb) * jax.nn.sigmoid(f_ref[0, rows, :])), 0.0)
        acc = None
        for part in _split_bf16(log_f, 2):
            t = _dot(tri_ref[...], part)
            acc = t if acc is None else acc + t
        b_ref[rows, :] = acc

    n_sub = CHUNK // HG_SUB
    causal = (lax.broadcasted_iota(jnp.int32, (CHUNK, CHUNK), 1)
              <= lax.broadcasted_iota(jnp.int32, (CHUNK, CHUNK), 0))

    def step(n, states):
        states = list(states)
        tiles = [(u, h) for u in range(unroll) for h in range(HG_HEADS)]
        rows = {u: pl.ds(pl.multiple_of((n * unroll + u) * CHUNK, CHUNK), CHUNK) for u in range(unroll)}
        cols = {h: slice(h * HG_KEY, (h + 1) * HG_KEY) for h in range(HG_HEADS)}

        work = {}
        for u, h in tiles:
            x = f_ref[0, rows[u], cols[h]]
            b = b_ref[rows[u], cols[h]]
            q_raw = qig_ref[0, rows[u], cols[h]].astype(F32)
            q = q_raw * jax.nn.sigmoid(q_raw)
            k = (1.0 - lb[:, cols[h]]) * jax.nn.sigmoid(-x)
            v16 = qig_ref[0, rows[u], width + h * HG_KEY:width + (h + 1) * HG_KEY]
            b_last = b[CHUNK - 1:CHUNK, :]
            qd = (q * jnp.exp(b)).astype(BF16)
            kd = (k * jnp.exp(b_last - b)).astype(BF16)
            update = lax.dot_general(v16, kd, TN_DIMS, preferred_element_type=F32)
            qf, kf = [], []
            for i in range(n_sub):
                lo, hi = i * HG_SUB, (i + 1) * HG_SUB
                b_ref_i = jnp.zeros((1, HG_KEY), F32) if i == 0 else b[lo - 1:lo, :]
                qf_i = q[lo:hi, :] * jnp.exp(b[lo:hi, :] - b_ref_i)
                above = [jnp.zeros((lo, HG_KEY), F32)] if lo else []
                below = [jnp.zeros((CHUNK - hi, HG_KEY), F32)] if hi < CHUNK else []
                qf.append(jnp.concatenate(above + [qf_i] + below, axis=0))
                kf.append(jnp.concatenate([k[:hi, :] * jnp.exp(b_ref_i - b[:hi, :])] + below, axis=0))
            a = lax.dot_general(jnp.concatenate(qf, axis=1).astype(BF16),
                                jnp.concatenate(kf, axis=1).astype(BF16), NT_DIMS,
                                preferred_element_type=F32)
            a = jnp.where(causal, a, 0.0).astype(BF16)
            work[u, h] = (qd, update, jnp.exp(b_last), a, v16)

        inter = {}
        for u, h in tiles:
            qd, update, decay, _, _ = work[u, h]
            inter[u, h] = lax.dot_general(qd, states[h].astype(BF16), NT_DIMS, preferred_element_type=F32)
            states[h] = states[h] * decay + update

        for u, h in tiles:
            _, _, _, a, v16 = work[u, h]
            o = inter[u, h] + _dot(a, v16)
            o = o * lax.rsqrt(jnp.mean(o * o, axis=-1, keepdims=True) + EPS) * nw_ref[:, cols[h]]
            g = qig_ref[0, rows[u], 2 * width + h * HG_KEY:2 * width + (h + 1) * HG_KEY].astype(F32)
            o_ref[0, rows[u], cols[h]] = (o * (g * jax.nn.sigmoid(g))).astype(o_ref.dtype)
        return tuple(states)

    zero = jnp.zeros((HG_KEY, HG_KEY), F32)
    lax.fori_loop(0, seq // (CHUNK * unroll), step, (zero,) * HG_HEADS)


def _hgrn(hg_f, hg_qig, lbp, nw, tri, layer):
    b, seq, width = hg_f.shape
    return pl.pallas_call(
        functools.partial(_hgrn_kernel, seq=seq, layer=layer, unroll=4),
        grid=(b,),
        in_specs=[pl.BlockSpec((1, seq, width), lambda i: (i, 0, 0)),
                  pl.BlockSpec((1, seq, 3 * width), lambda i: (i, 0, 0)),
                  _resident(lbp.shape), _resident(nw.shape), _resident(tri.shape)],
        out_specs=pl.BlockSpec((1, seq, width), lambda i: (i, 0, 0)),
        out_shape=jax.ShapeDtypeStruct((b, seq, width), BF16),
        scratch_shapes=[pltpu.VMEM((seq, width), F32)],
        compiler_params=_params(1),
        name="hgrn2",
    )(hg_f, hg_qig, lbp, nw, tri)


def _rot_half(w):
    half = MLA_ROPE // 2
    return jnp.concatenate([-w[..., half:], w[..., :half]], axis=-1)


def _lower(n):
    r = lax.broadcasted_iota(jnp.int32, (n, n), 0)
    c = lax.broadcasted_iota(jnp.int32, (n, n), 1)
    return r >= c


def kernel(x, p, positions, ffn_a_norm, ffn_a_w_in, ffn_a_w_out, mix_norm, w_in, mla_q_norm, mla_w_uq, mla_kv_norm, mla_w_ukv, hgrn_lower_bounds, hgrn_out_norm, w_br_mla, w_br_sb, w_br_hgrn, w_out, ffn_b_norm, ffn_b_w_in, ffn_b_w_out, ple_norm, w_ple_gate, w_ple_proj, final_norm):
    b, seq, d = x.shape
    depth = ffn_a_norm.shape[0]
    t = b * seq
    bf = lambda a: a.astype(BF16)
    row = lambda a: a.reshape(1, -1).astype(F32)

    tq = min(ATT_BLOCK, seq)
    sb_tri = _lower(tq).astype(BF16)
    gb = 4 * CHUNK
    r = lax.broadcasted_iota(jnp.int32, (gb, gb), 0)
    c = lax.broadcasted_iota(jnp.int32, (gb, gb), 1)
    hg_tri = ((c <= r) & (c // CHUNK == r // CHUNK)).astype(BF16)

    half = MLA_ROPE // 2
    inv = ROPE_BASE ** (-jnp.arange(half, dtype=F32) / half)
    inv = jnp.tile(inv, LANES // half).reshape(1, LANES)
    cos, sin = _rope_tables(positions.reshape(b, seq, 1), inv)

    splits = (MLA_Q_LORA, MLA_KV_LORA, MLA_ROPE, 3 * SB_HEADS * SB_HEAD_DIM,
              4 * HG_HEADS * HG_KEY, N_BRANCH * d)
    hg_w = HG_HEADS * HG_KEY
    widths = (MLA_IN_WIDTH, splits[3], hg_w, 3 * hg_w, splits[5])

    h = x.reshape(t, d)
    for i in range(depth):
        h = _ffn(h, row(ffn_a_norm[i]), _layer_bf16(ffn_a_w_in, i), _layer_bf16(ffn_a_w_out, i))

        w = w_in[i]
        offs = [0]
        for n in splits:
            offs.append(offs[-1] + n)
        w_cq, w_ckv, w_kr, w_sb, w_hg, w_gate = (w[:, offs[j]:offs[j + 1]] for j in range(6))
        n_sbq = SB_HEADS * SB_HEAD_DIM
        w_sb = jnp.concatenate([w_sb[:, :n_sbq] * (SB_HEAD_DIM ** -0.5 * LOG2E), w_sb[:, n_sbq:]], axis=1)
        w_hq, w_hf, w_hi, w_hgate = (w_hg[:, j * hg_w:(j + 1) * hg_w] for j in range(4))
        w_cat = bf(jnp.concatenate([w_cq, w_ckv, jnp.tile(w_kr, (1, 4)), jnp.tile(_rot_half(w_kr), (1, 4)),
                                    w_sb, w_hf, w_hq, w_hi, w_hgate, w_gate], axis=1))
        mla_in, sb_in, hg_f, hg_qig, gates = _inproj(h, row(mix_norm[i]), w_cat, widths,
                                                     (BF16, BF16, F32, BF16, BF16))

        wq = mla_w_uq[i].reshape(MLA_Q_LORA, MLA_HEADS, MLA_NOPE + MLA_ROPE)
        wq_nope = bf(wq[:, :, :MLA_NOPE].reshape(MLA_Q_LORA, -1))
        wq_rope = wq[:, :, MLA_NOPE:]
        wkv = mla_w_ukv[i].reshape(MLA_KV_LORA, MLA_HEADS, MLA_NOPE + MLA_V)
        y_a, y_b = _mla_sb(
            mla_in.reshape(b, seq, -1), cos, sin, row(mla_q_norm[i]), row(mla_kv_norm[i]),
            wq_nope, bf(wq_rope.reshape(MLA_Q_LORA, -1)), bf(_rot_half(wq_rope).reshape(MLA_Q_LORA, -1)),
            bf(wkv[:, :, :MLA_NOPE].reshape(MLA_KV_LORA, -1)), bf(wkv[:, :, MLA_NOPE:].reshape(MLA_KV_LORA, -1)),
            sb_in.reshape(b, seq, -1), sb_tri)
        y_c = _hgrn(hg_f.reshape(b, seq, -1), hg_qig.reshape(b, seq, -1), hgrn_lower_bounds.astype(F32),
                    row(hgrn_out_norm[i]), hg_tri, i)

        h = _ffn(h, row(ffn_b_norm[i]), _layer_bf16(ffn_b_w_in, i), _layer_bf16(ffn_b_w_out, i),
                 mix=(y_a.reshape(t, -1), y_b.reshape(t, -1), y_c.reshape(t, -1), gates, _layer_bf16(w_br_mla, i),
                      _layer_bf16(w_br_sb, i), _layer_bf16(w_br_hgrn, i), _layer_bf16(w_out, i)),
                 embed=(p.reshape(depth, t, -1), i, row(ple_norm[i]), _layer_bf16(w_ple_gate, i),
                        _layer_bf16(w_ple_proj, i), row(final_norm)),
                 final=(i == depth - 1))
    return h.reshape(b, seq, d)
```

```python
import functools

import jax
import jax.numpy as jnp
from jax import lax
from jax.experimental import pallas as pl
from jax.experimental.pallas import tpu as pltpu

F32 = jnp.float32
BF16 = jnp.bfloat16

EPS = 1e-6
LB_FLOOR = 1e-30
CHUNK = 64
HG_SUB = 16
MLA_HEADS = 8
MLA_NOPE = 64
MLA_ROPE = 32
MLA_V = 64
MLA_Q_LORA = 384
MLA_KV_LORA = 256
ROPE_BASE = 10000.0
LOG2E = 1.4426950408889634
INV_LN2 = LOG2E
SB_DEAD_LOG2 = 160.0
SB_HEADS = 8
SB_HEAD_DIM = 64
HG_HEADS = 4
HG_KEY = 128
N_BRANCH = 3

LANES = 128
MLA_IN_WIDTH = MLA_Q_LORA + MLA_KV_LORA + 2 * LANES
ATT_BLOCK = 256
ROW_BLOCK = 512
CAST_ROWS = 256
VMEM_LIMIT_BYTES = 56 * 1024 * 1024

NT_DIMS = (((1,), (1,)), ((), ()))
TN_DIMS = (((0,), (0,)), ((), ()))


def _rms(x, w):
    return x * lax.rsqrt(jnp.mean(x * x, axis=-1, keepdims=True) + EPS) * w


def _dot(a, b):
    return jnp.dot(a, b, preferred_element_type=F32)


def _softplus(z):
    return jnp.maximum(z, 0.0) + jnp.log1p(jnp.exp(-jnp.abs(z)))


def _split_bf16(x, terms):
    parts = []
    for _ in range(terms - 1):
        hi = x.astype(BF16)
        parts.append(hi)
        x = x - hi.astype(F32)
    parts.append(x.astype(BF16))
    return parts


def _rot_half(w):
    half = MLA_ROPE // 2
    return jnp.concatenate([-w[..., half:], w[..., :half]], axis=-1)


def _params(n_axes):
    return pltpu.CompilerParams(dimension_semantics=("arbitrary",) * n_axes,
                                vmem_limit_bytes=VMEM_LIMIT_BYTES)


def _resident(shape):
    return pl.BlockSpec(shape, lambda *_: (0,) * len(shape), pipeline_mode=pl.Buffered(1))


def _cast_kernel(w_ref, o_ref):
    o_ref[...] = w_ref[0].astype(o_ref.dtype)


def _layer_bf16(w, layer):
    _, k, n = w.shape
    tk = CAST_ROWS if k % CAST_ROWS == 0 else k
    return pl.pallas_call(
        _cast_kernel,
        grid=(k // tk,),
        in_specs=[pl.BlockSpec((1, tk, n), lambda r: (layer, r, 0))],
        out_specs=pl.BlockSpec((tk, n), lambda r: (r, 0)),
        out_shape=jax.ShapeDtypeStruct((k, n), BF16),
        compiler_params=_params(1),
        name="to_bf16",
    )(w)


def _inproj_weight_kernel(w_ref, o_ref):
    w = w_ref[0]
    c_lat = MLA_Q_LORA + MLA_KV_LORA
    n_sb, n_hg = SB_HEADS * SB_HEAD_DIM, HG_HEADS * HG_KEY
    kr = w[:, c_lat:c_lat + MLA_ROPE]
    sb = w[:, c_lat + MLA_ROPE:c_lat + MLA_ROPE + 3 * n_sb]
    hg = w[:, c_lat + MLA_ROPE + 3 * n_sb:c_lat + MLA_ROPE + 3 * n_sb + 4 * n_hg]
    gates = w[:, c_lat + MLA_ROPE + 3 * n_sb + 4 * n_hg:]
    copies = LANES // MLA_ROPE
    parts = ([w[:, :c_lat]] + [kr] * copies + [_rot_half(kr)] * copies
             + [sb[:, :n_sb] * (SB_HEAD_DIM ** -0.5 * LOG2E), sb[:, n_sb:]]
             + [hg[:, n_hg:2 * n_hg], hg[:, :n_hg], hg[:, 2 * n_hg:]]
             + [gates])
    o_ref[...] = jnp.concatenate(parts, axis=1).astype(o_ref.dtype)


def _inproj_weight(w, layer, width):
    _, k, n = w.shape
    return pl.pallas_call(
        _inproj_weight_kernel,
        grid=(k // CAST_ROWS,),
        in_specs=[pl.BlockSpec((1, CAST_ROWS, n), lambda r: (layer, r, 0))],
        out_specs=pl.BlockSpec((CAST_ROWS, width), lambda r: (r, 0)),
        out_shape=jax.ShapeDtypeStruct((k, width), BF16),
        compiler_params=_params(1),
        name="inproj_weight",
    )(w)


def _ffn_kernel(*refs, d_ff, fc, mix, embed, final):
    refs = list(refs)
    h_ref = refs.pop(0)
    if mix:
        ya_ref, yb_ref, yc_ref, gate_ref, wa_ref, wb_ref, wc_ref, wo_ref = refs[:8]
        del refs[:8]
    nw_ref, win_ref, wout_ref = refs[:3]
    del refs[:3]
    if embed:
        p_ref, pn_ref, wg_ref, wp_ref, fw_ref = refs[:5]
        del refs[:5]
    o_ref, xn_ref, hid_ref = refs

    x = h_ref[...]
    if mix:
        d = x.shape[1]
        merged = None
        for j, (y_ref, w_ref) in enumerate(((ya_ref, wa_ref), (yb_ref, wb_ref), (yc_ref, wc_ref))):
            gate = jax.nn.sigmoid(gate_ref[:, j * d:(j + 1) * d].astype(F32))
            term = gate * _dot(y_ref[...], w_ref[...])
            merged = term if merged is None else merged + term
        x = x + _dot(merged.astype(BF16), wo_ref[...])
    xn_ref[...] = _rms(x, nw_ref[...]).astype(BF16)
    for c in range(d_ff // fc):
        xn = xn_ref[...]
        g = _dot(xn, win_ref[:, c * fc:(c + 1) * fc])
        up = _dot(xn, win_ref[:, d_ff + c * fc:d_ff + (c + 1) * fc])
        hid_ref[:, c * fc:(c + 1) * fc] = (g * jax.nn.sigmoid(g) * up).astype(BF16)
    h = x + 0.5 * _dot(hid_ref[...], wout_ref[...])
    if embed:
        gate = jax.nn.sigmoid(_dot(_rms(h, pn_ref[...]).astype(BF16), wg_ref[...]))
        h = h + _dot(p_ref[0].astype(BF16), wp_ref[...]) * gate
        if final:
            h = _rms(h, fw_ref[...])
    o_ref[...] = h


def _ffn(h, nw, w_in, w_out, mix=None, embed=None, final=False):
    t, d = h.shape
    d_ff = w_out.shape[0]
    tm = min(ROW_BLOCK, t)
    row = lambda n: pl.BlockSpec((tm, n), lambda i: (i, 0))
    in_specs, args = [row(d)], [h]
    if mix is not None:
        in_specs += [row(a.shape[1]) for a in mix[:4]] + [_resident(w.shape) for w in mix[4:]]
        args += list(mix)
    in_specs += [_resident((1, d)), _resident((d, 2 * d_ff)), _resident((d_ff, d))]
    args += [nw, w_in, w_out]
    if embed is not None:
        p, layer, pn, wg, wp, fw = embed
        in_specs += [pl.BlockSpec((1, tm, p.shape[2]), lambda i: (layer, i, 0)),
                     _resident(pn.shape), _resident(wg.shape), _resident(wp.shape), _resident(fw.shape)]
        args += [p, pn, wg, wp, fw]
    return pl.pallas_call(
        functools.partial(_ffn_kernel, d_ff=d_ff, fc=256, mix=mix is not None, embed=embed is not None,
                          final=final),
        grid=(t // tm,),
        in_specs=in_specs,
        out_specs=row(d),
        out_shape=jax.ShapeDtypeStruct((t, d), F32),
        scratch_shapes=[pltpu.VMEM((tm, d), BF16), pltpu.VMEM((tm, d_ff), BF16)],
        compiler_params=_params(1),
        name="ffn",
    )(*args)


def _inproj_kernel(h_ref, nw_ref, w_ref, *o_refs, widths, nc):
    u = _rms(h_ref[...], nw_ref[...]).astype(BF16)
    start = 0
    for o_ref, width in zip(o_refs, widths):
        for c in range(0, width, nc):
            n = min(nc, width - c)
            o_ref[:, c:c + n] = _dot(u, w_ref[:, start + c:start + c + n]).astype(o_ref.dtype)
        start += width


def _inproj(h, nw, w, widths, dtypes):
    t, d = h.shape
    tm = min(ROW_BLOCK, t)
    return pl.pallas_call(
        functools.partial(_inproj_kernel, widths=widths, nc=256),
        grid=(t // tm,),
        in_specs=[pl.BlockSpec((tm, d), lambda i: (i, 0)), _resident((1, d)), _resident(w.shape)],
        out_specs=[pl.BlockSpec((tm, n), lambda i: (i, 0)) for n in widths],
        out_shape=[jax.ShapeDtypeStruct((t, n), dt) for n, dt in zip(widths, dtypes)],
        compiler_params=_params(1),
        name="inproj",
    )(h, nw, w)


def _rope_kernel(pos_ref, inv_ref, cos_ref, sin_ref):
    ang = pos_ref[0].astype(F32) * inv_ref[...]
    cos_ref[0] = jnp.cos(ang)
    sin_ref[0] = jnp.sin(ang)


def _rope_tables(pos, inv):
    b, seq, _ = pos.shape
    out = jax.ShapeDtypeStruct((b, seq, LANES), F32)
    spec = pl.BlockSpec((1, seq, LANES), lambda i: (i, 0, 0))
    return pl.pallas_call(
        _rope_kernel,
        grid=(b,),
        in_specs=[pl.BlockSpec((1, seq, 1), lambda i: (i, 0, 0)), _resident(inv.shape)],
        out_specs=[spec, spec],
        out_shape=[out, out],
        compiler_params=_params(1),
        name="rope_tables",
    )(pos, inv)


def _mla_kernel(x_ref, cos_ref, sin_ref, qn_ref, kvn_ref, wqn_ref, wqr_ref, wqrp_ref, wk_ref, wv_ref,
                o_ref, qcat_ref, kcat_ref, v_ref, *, seq, tq, scale):
    p = pl.program_id(1)
    n_pairs = MLA_HEADS // 2

    @pl.when(p == 0)
    def _project():
        rb = min(ROW_BLOCK, seq)
        for r in range(seq // rb):
            rows = slice(r * rb, (r + 1) * rb)
            x = x_ref[0, rows, :]
            c_q = x[:, :MLA_Q_LORA].astype(F32)
            c_kv = x[:, MLA_Q_LORA:MLA_Q_LORA + MLA_KV_LORA].astype(F32)
            kr = x[:, MLA_Q_LORA + MLA_KV_LORA:MLA_Q_LORA + MLA_KV_LORA + LANES].astype(F32)
            krp = x[:, MLA_Q_LORA + MLA_KV_LORA + LANES:].astype(F32)
            cos, sin = cos_ref[0, rows, :], sin_ref[0, rows, :]
            cqn = _rms(c_q, qn_ref[...]).astype(BF16)
            ckvn = _rms(c_kv, kvn_ref[...]).astype(BF16)
            k_rope = (kr * cos + krp * sin).astype(BF16)
            halves = lambda a: (a[:, :LANES], a[:, LANES:])
            ropes = zip(halves(_dot(cqn, wqr_ref[...])), halves(_dot(cqn, wqrp_ref[...])))
            for g, (qr, qrp) in enumerate(ropes):
                q_rope = ((qr * cos + qrp * sin) * scale).astype(BF16)
                qcat_ref[2 * g, rows, LANES:] = q_rope
                qcat_ref[2 * g + 1, rows, LANES:] = q_rope
            for g in range(n_pairs // 2):
                cols = slice(2 * g * LANES, (2 * g + 2) * LANES)
                parts = zip(halves(_dot(cqn, wqn_ref[:, cols])), halves(_dot(ckvn, wk_ref[:, cols])),
                            halves(_dot(ckvn, wv_ref[:, cols])))
                for pp, (qn, kn, vv) in enumerate(parts, start=2 * g):
                    qcat_ref[pp, rows, :LANES] = (qn * scale).astype(BF16)
                    kcat_ref[pp, rows, :LANES] = kn.astype(BF16)
                    kcat_ref[pp, rows, LANES:] = k_rope
                    v_ref[pp, rows, :] = vv.astype(BF16)

    lane = lax.broadcasted_iota(jnp.int32, (1, 2 * LANES), 1)
    out_lane = lax.broadcasted_iota(jnp.int32, (1, LANES), 1)
    ri = lax.broadcasted_iota(jnp.int32, (tq, tq), 0)
    ci = lax.broadcasted_iota(jnp.int32, (tq, tq), 1)
    diag_mask = (ci // CHUNK) <= (ri // CHUNK)
    group = jnp.where(lane < LANES, lane // MLA_NOPE, 2 + (lane - LANES) // MLA_ROPE)

    for n in range(seq // tq):
        keys = [slice(j * tq, (j + 1) * tq) for j in range(n + 1)]
        qc = qcat_ref[p, keys[n], :]
        s = []
        for hh in range(2):
            own = (group == hh) | (group == 2 + (p % 2) * 2 + hh)
            qh = qc * jnp.where(own, 1.0, 0.0).astype(BF16)
            sh = [lax.dot_general(qh, kcat_ref[p, kj, :], NT_DIMS, preferred_element_type=F32)
                  for kj in keys]
            sh[n] = jnp.where(diag_mask, sh[n], -jnp.inf)
            s.append(sh)
        m = [jnp.max(functools.reduce(jnp.maximum, sh), axis=-1, keepdims=True) for sh in s]
        outs = []
        for hh in range(2):
            pr = [jnp.exp2(sj - m[hh]) for sj in s[hh]]
            l = jnp.sum(functools.reduce(jnp.add, pr), axis=-1, keepdims=True)
            acc = None
            for j, kj in enumerate(keys):
                t = _dot(pr[j].astype(BF16), v_ref[p, kj, :])
                acc = t if acc is None else acc + t
            outs.append(acc * (1.0 / l))
        o_ref[0, keys[n], :] = jnp.where(out_lane < MLA_V, outs[0], outs[1]).astype(o_ref.dtype)


def _sb_kernel(q_ref, k_ref, v_ref, tri_ref, o_ref, *, tq, n_q):
    lane = lax.broadcasted_iota(jnp.int32, (1, LANES), 1)
    ri = lax.broadcasted_iota(jnp.int32, (tq, tq), 0)
    ci = lax.broadcasted_iota(jnp.int32, (tq, tq), 1)
    diag_mask = ci < ri
    sign = jnp.int32(-2 ** 31)
    rows = [slice(j * tq, (j + 1) * tq) for j in range(n_q)]

    def sweep(blocks, state):
        tiles = [(n, hh, j) for n, js in blocks.items() for j in js for hh in range(2)]
        qhs = {}
        for n in blocks:
            q = q_ref[0, rows[n], :]
            for hh in range(2):
                qhs[n, hh] = jnp.where((lane // SB_HEAD_DIM) == hh, q, jnp.zeros_like(q))
        z2 = {t: lax.dot_general(qhs[t[0], t[1]], k_ref[0, rows[t[2]], :], NT_DIMS,
                                 preferred_element_type=F32) for t in tiles}
        suffix, total = {}, {}
        for t in tiles:
            neg_abs = lax.bitcast_convert_type(lax.bitcast_convert_type(z2[t], jnp.int32) | sign, F32)
            sp2 = jnp.maximum(z2[t], 0.0) + jnp.log(1.0 + jnp.exp2(neg_abs)) * INV_LN2
            keep = jnp.where(diag_mask, sp2, 0.0) if t[2] == t[0] else sp2
            suffix[t] = _dot(keep.astype(BF16), tri_ref[...])
            total[t] = jnp.sum(keep, axis=-1, keepdims=True)
        state = dict(state)
        for n, hh, j in tiles:
            rest, acc = state[n, hh]
            a = jnp.exp2(jnp.minimum(z2[n, hh, j] - suffix[n, hh, j] - rest, 0.0))
            if j == n:
                a = jnp.where(diag_mask, a, 0.0)
            pv = _dot(a.astype(BF16), v_ref[0, rows[j], :])
            state[n, hh] = (rest + total[n, hh, j], pv if acc is None else acc + pv)
        return state

    def store(n, state):
        o_ref[0, rows[n], :] = jnp.where(lane < SB_HEAD_DIM, state[n, 0][1], state[n, 1][1]).astype(o_ref.dtype)

    start = {(n, hh): (jnp.zeros((tq, 1), F32), None) for n in range(n_q) for hh in range(2)}
    near = sweep({n: [j for j in (n, n - 1) if j >= 0] for n in range(n_q)}, start)
    for n in range(n_q):
        store(n, near)

    for n in range(2, n_q):
        alive = jnp.minimum(jnp.min(near[n, 0][0]), jnp.min(near[n, 1][0])) < SB_DEAD_LOG2

        @pl.when(alive)
        def _(n=n):
            far = sweep({n: list(range(n - 2, -1, -1))}, {k: v for k, v in near.items() if k[0] == n})
            store(n, far)


def _mla_sb_kernel(x_ref, cos_ref, sin_ref, qn_ref, kvn_ref, wqn_ref, wqr_ref, wqrp_ref, wk_ref, wv_ref,
                   sq_ref, sk_ref, sv_ref, tri_ref, oa_ref, ob_ref, qcat_ref, kcat_ref, v_ref, *, seq, tq, scale):
    _mla_kernel(x_ref, cos_ref, sin_ref, qn_ref, kvn_ref, wqn_ref, wqr_ref, wqrp_ref, wk_ref, wv_ref,
                oa_ref, qcat_ref, kcat_ref, v_ref, seq=seq, tq=tq, scale=scale)
    _sb_kernel(sq_ref, sk_ref, sv_ref, tri_ref, ob_ref, tq=tq, n_q=seq // tq)


def _mla_sb(x, cos, sin, qn, kvn, wqn, wqr, wqrp, wk, wv, qkv, tri):
    b, seq, _ = x.shape
    assert MLA_HEADS == SB_HEADS and MLA_V == SB_HEAD_DIM
    n_pairs = MLA_HEADS // 2
    tq = min(ATT_BLOCK, seq)
    scale = float((MLA_NOPE + MLA_ROPE) ** -0.5) * LOG2E
    col = lambda off: pl.BlockSpec((1, seq, LANES), lambda i, j: (i, 0, off + j))
    full = lambda n: pl.BlockSpec((1, seq, n), lambda i, j: (i, 0, 0))
    out = jax.ShapeDtypeStruct((b, seq, MLA_HEADS * MLA_V), BF16)
    return pl.pallas_call(
        functools.partial(_mla_sb_kernel, seq=seq, tq=tq, scale=scale),
        grid=(b, n_pairs),
        in_specs=[full(MLA_IN_WIDTH), full(LANES), full(LANES),
                  _resident(qn.shape), _resident(kvn.shape),
                  _resident(wqn.shape), _resident(wqr.shape), _resident(wqrp.shape),
                  _resident(wk.shape), _resident(wv.shape),
                  col(0), col(n_pairs), col(2 * n_pairs), _resident(tri.shape)],
        out_specs=[col(0), col(0)],
        out_shape=[out, out],
        scratch_shapes=[pltpu.VMEM((n_pairs, seq, 2 * LANES), BF16),
                        pltpu.VMEM((n_pairs, seq, 2 * LANES), BF16),
                        pltpu.VMEM((n_pairs, seq, LANES), BF16)],
        compiler_params=_params(2),
        name="mla_sb",
    )(x, cos, sin, qn, kvn, wqn, wqr, wqrp, wk, wv, qkv, qkv, qkv, tri)


def _hgrn_kernel(f_ref, qig_ref, lbp_ref, nw_ref, tri_ref, o_ref, b_ref, *, seq, layer, unroll):
    width = HG_HEADS * HG_KEY
    lbp = lbp_ref[...]
    e = jnp.exp(lbp - jnp.max(lbp, axis=0, keepdims=True))
    sm = e / jnp.sum(e, axis=0, keepdims=True)
    lb = jnp.zeros((1, width), F32)
    for j in range(1, layer + 1):
        lb = lb + sm[j:j + 1, :]
    lb = jnp.clip(lb, 0.0, 1.0 - 1e-6)
    lb_floor = jnp.maximum(lb, LB_FLOOR)

    gb = tri_ref.shape[0]
    for r in range(seq // gb):
        rows = slice(r * gb, (r + 1) * gb)
        log_f = jnp.minimum(jnp.log(lb_floor + (1.0 - lb) * jax.nn.sigmoid(f_ref[0, rows, :])), 0.0)
        acc = None
        for part in _split_bf16(log_f * LOG2E, 2):
            t = _dot(tri_ref[...], part)
            acc = t if acc is None else acc + t
        b_ref[rows, :] = acc

    n_sub = CHUNK // HG_SUB
    causal = (lax.broadcasted_iota(jnp.int32, (CHUNK, CHUNK), 1)
              <= lax.broadcasted_iota(jnp.int32, (CHUNK, CHUNK), 0))

    def step(n, states):
        states = list(states)
        tiles = [(u, h) for u in range(unroll) for h in range(HG_HEADS)]
        rows = {u: pl.ds(pl.multiple_of((n * unroll + u) * CHUNK, CHUNK), CHUNK) for u in range(unroll)}
        cols = {h: slice(h * HG_KEY, (h + 1) * HG_KEY) for h in range(HG_HEADS)}

        work = {}
        for u, h in tiles:
            x = f_ref[0, rows[u], cols[h]]
            b = b_ref[rows[u], cols[h]]
            q_raw = qig_ref[0, rows[u], cols[h]].astype(F32)
            q = q_raw * jax.nn.sigmoid(q_raw)
            k = (1.0 - lb[:, cols[h]]) * jax.nn.sigmoid(-x)
            v16 = qig_ref[0, rows[u], width + h * HG_KEY:width + (h + 1) * HG_KEY]
            b_last = b[CHUNK - 1:CHUNK, :]
            qd = (q * jnp.exp2(b)).astype(BF16)
            kd = (k * jnp.exp2(b_last - b)).astype(BF16)
            update = lax.dot_general(v16, kd, TN_DIMS, preferred_element_type=F32)
            qf, kf = [], []
            for i in range(n_sub):
                lo, hi = i * HG_SUB, (i + 1) * HG_SUB
                b_ref_i = jnp.zeros((1, HG_KEY), F32) if i == 0 else b[lo - 1:lo, :]
                qf_i = q[lo:hi, :] * jnp.exp2(b[lo:hi, :] - b_ref_i)
                above = [jnp.zeros((lo, HG_KEY), F32)] if lo else []
                below = [jnp.zeros((CHUNK - hi, HG_KEY), F32)] if hi < CHUNK else []
                qf.append(jnp.concatenate(above + [qf_i] + below, axis=0))
                kf.append(jnp.concatenate([k[:hi, :] * jnp.exp2(b_ref_i - b[:hi, :])] + below, axis=0))
            a = lax.dot_general(jnp.concatenate(qf, axis=1).astype(BF16),
                                jnp.concatenate(kf, axis=1).astype(BF16), NT_DIMS,
                                preferred_element_type=F32)
            a = jnp.where(causal, a, 0.0).astype(BF16)
            work[u, h] = (qd, update, jnp.exp2(b_last), a, v16)

        inter = {}
        for u, h in tiles:
            qd, update, decay, _, _ = work[u, h]
            inter[u, h] = lax.dot_general(qd, states[h].astype(BF16), NT_DIMS, preferred_element_type=F32)
            states[h] = states[h] * decay + update

        for u, h in tiles:
            _, _, _, a, v16 = work[u, h]
            o = inter[u, h] + _dot(a, v16)
            o = o * lax.rsqrt(jnp.mean(o * o, axis=-1, keepdims=True) + EPS) * nw_ref[:, cols[h]]
            g = qig_ref[0, rows[u], 2 * width + h * HG_KEY:2 * width + (h + 1) * HG_KEY].astype(F32)
            o_ref[0, rows[u], cols[h]] = (o * (g * jax.nn.sigmoid(g))).astype(o_ref.dtype)
        return tuple(states)

    zero = jnp.zeros((HG_KEY, HG_KEY), F32)
    lax.fori_loop(0, seq // (CHUNK * unroll), step, (zero,) * HG_HEADS)


def _hgrn(hg_f, hg_qig, lbp, nw, tri, layer):
    b, seq, width = hg_f.shape
    return pl.pallas_call(
        functools.partial(_hgrn_kernel, seq=seq, layer=layer, unroll=4),
        grid=(b,),
        in_specs=[pl.BlockSpec((1, seq, width), lambda i: (i, 0, 0)),
                  pl.BlockSpec((1, seq, 3 * width), lambda i: (i, 0, 0)),
                  _resident(lbp.shape), _resident(nw.shape), _resident(tri.shape)],
        out_specs=pl.BlockSpec((1, seq, width), lambda i: (i, 0, 0)),
        out_shape=jax.ShapeDtypeStruct((b, seq, width), BF16),
        scratch_shapes=[pltpu.VMEM((seq, width), F32)],
        compiler_params=_params(1),
        name="hgrn2",
    )(hg_f, hg_qig, lbp, nw, tri)


def _lower(n):
    r = lax.broadcasted_iota(jnp.int32, (n, n), 0)
    c = lax.broadcasted_iota(jnp.int32, (n, n), 1)
    return r >= c


def kernel(x, p, positions, ffn_a_norm, ffn_a_w_in, ffn_a_w_out, mix_norm, w_in, mla_q_norm, mla_w_uq, mla_kv_norm, mla_w_ukv, hgrn_lower_bounds, hgrn_out_norm, w_br_mla, w_br_sb, w_br_hgrn, w_out, ffn_b_norm, ffn_b_w_in, ffn_b_w_out, ple_norm, w_ple_gate, w_ple_proj, final_norm):
    b, seq, d = x.shape
    depth = ffn_a_norm.shape[0]
    t = b * seq
    bf = lambda a: a.astype(BF16)
    row = lambda a: a.reshape(1, -1).astype(F32)

    tq = min(ATT_BLOCK, seq)
    sb_tri = _lower(tq).astype(BF16)
    gb = 4 * CHUNK
    r = lax.broadcasted_iota(jnp.int32, (gb, gb), 0)
    c = lax.broadcasted_iota(jnp.int32, (gb, gb), 1)
    hg_tri = ((c <= r) & (c // CHUNK == r // CHUNK)).astype(BF16)

    half = MLA_ROPE // 2
    inv = ROPE_BASE ** (-jnp.arange(half, dtype=F32) / half)
    inv = jnp.tile(inv, LANES // half).reshape(1, LANES)
    cos, sin = _rope_tables(positions.reshape(b, seq, 1), inv)

    hg_w = HG_HEADS * HG_KEY
    widths = (MLA_IN_WIDTH, 3 * SB_HEADS * SB_HEAD_DIM, hg_w, 3 * hg_w, N_BRANCH * d)

    h = x.reshape(t, d)
    for i in range(depth):
        h = _ffn(h, row(ffn_a_norm[i]), _layer_bf16(ffn_a_w_in, i), _layer_bf16(ffn_a_w_out, i))

        w_cat = _inproj_weight(w_in, i, sum(widths))
        mla_in, sb_in, hg_f, hg_qig, gates = _inproj(h, row(mix_norm[i]), w_cat, widths,
                                                     (BF16, BF16, F32, BF16, BF16))

        wq = mla_w_uq[i].reshape(MLA_Q_LORA, MLA_HEADS, MLA_NOPE + MLA_ROPE)
        wq_nope = bf(wq[:, :, :MLA_NOPE].reshape(MLA_Q_LORA, -1))
        wq_rope = wq[:, :, MLA_NOPE:]
        wkv = mla_w_ukv[i].reshape(MLA_KV_LORA, MLA_HEADS, MLA_NOPE + MLA_V)
        y_a, y_b = _mla_sb(
            mla_in.reshape(b, seq, -1), cos, sin, row(mla_q_norm[i]), row(mla_kv_norm[i]),
            wq_nope, bf(wq_rope.reshape(MLA_Q_LORA, -1)), bf(_rot_half(wq_rope).reshape(MLA_Q_LORA, -1)),
            bf(wkv[:, :, :MLA_NOPE].reshape(MLA_KV_LORA, -1)), bf(wkv[:, :, MLA_NOPE:].reshape(MLA_KV_LORA, -1)),
            sb_in.reshape(b, seq, -1), sb_tri)
        y_c = _hgrn(hg_f.reshape(b, seq, -1), hg_qig.reshape(b, seq, -1), hgrn_lower_bounds.astype(F32),
                    row(hgrn_out_norm[i]), hg_tri, i)

        h = _ffn(h, row(ffn_b_norm[i]), _layer_bf16(ffn_b_w_in, i), _layer_bf16(ffn_b_w_out, i),
                 mix=(y_a.reshape(t, -1), y_b.reshape(t, -1), y_c.reshape(t, -1), gates, _layer_bf16(w_br_mla, i),
                      _layer_bf16(w_br_sb, i), _layer_bf16(w_br_hgrn, i), _layer_bf16(w_out, i)),
                 embed=(p.reshape(depth, t, -1), i, row(ple_norm[i]), _layer_bf16(w_ple_gate, i),
                        _layer_bf16(w_ple_proj, i), row(final_norm)),
                 final=(i == depth - 1))
    return h.reshape(b, seq, d)
```

```python
import functools

import jax
import jax.numpy as jnp
from jax import lax
from jax.experimental import pallas as pl
from jax.experimental.pallas import tpu as pltpu

F32 = jnp.float32
BF16 = jnp.bfloat16

EPS = 1e-6
LB_FLOOR = 1e-30
CHUNK = 64
HG_SUB = 16
MLA_HEADS = 8
MLA_NOPE = 64
MLA_ROPE = 32
MLA_V = 64
MLA_Q_LORA = 384
MLA_KV_LORA = 256
ROPE_BASE = 10000.0
LOG2E = 1.4426950408889634
INV_LN2 = LOG2E
SB_DEAD_LOG2 = 160.0
SB_HEADS = 8
SB_HEAD_DIM = 64
HG_HEADS = 4
HG_KEY = 128
N_BRANCH = 3

LANES = 128
MLA_IN_WIDTH = MLA_Q_LORA + MLA_KV_LORA + 2 * LANES
MXU_WIDTH = 256
ATT_BLOCK = MXU_WIDTH
FFN_CHUNK = MXU_WIDTH
PROJ_CHUNK = MXU_WIDTH
ROW_BLOCK = 512
CAST_ROWS = 256
HG_UNROLL = 8
VMEM_LIMIT_BYTES = 56 * 1024 * 1024

NT_DIMS = (((1,), (1,)), ((), ()))
TN_DIMS = (((0,), (0,)), ((), ()))


def _rms(x, w):
    return x * lax.rsqrt(jnp.mean(x * x, axis=-1, keepdims=True) + EPS) * w


def _dot(a, b):
    return jnp.dot(a, b, preferred_element_type=F32)


def _split_bf16(x, terms):
    parts = []
    for _ in range(terms - 1):
        hi = x.astype(BF16)
        parts.append(hi)
        x = x - hi.astype(F32)
    parts.append(x.astype(BF16))
    return parts


def _rot_half(w):
    half = MLA_ROPE // 2
    return jnp.concatenate([-w[..., half:], w[..., :half]], axis=-1)


def _params(n_axes):
    return pltpu.CompilerParams(dimension_semantics=("arbitrary",) * n_axes,
                                vmem_limit_bytes=VMEM_LIMIT_BYTES)


def _resident(shape):
    return pl.BlockSpec(shape, lambda *_: (0,) * len(shape), pipeline_mode=pl.Buffered(1))


def _cast_kernel(w_ref, o_ref):
    o_ref[...] = w_ref[0].astype(o_ref.dtype)


def _layer_bf16(w, layer):
    _, k, n = w.shape
    tk = CAST_ROWS if k % CAST_ROWS == 0 else k
    return pl.pallas_call(
        _cast_kernel,
        grid=(k // tk,),
        in_specs=[pl.BlockSpec((1, tk, n), lambda r: (layer, r, 0))],
        out_specs=pl.BlockSpec((tk, n), lambda r: (r, 0)),
        out_shape=jax.ShapeDtypeStruct((k, n), BF16),
        compiler_params=_params(1),
        name="to_bf16",
    )(w)


def _inproj_weight_kernel(w_ref, o_ref):
    w = w_ref[0]
    c_lat = MLA_Q_LORA + MLA_KV_LORA
    n_sb, n_hg = SB_HEADS * SB_HEAD_DIM, HG_HEADS * HG_KEY
    kr = w[:, c_lat:c_lat + MLA_ROPE]
    sb = w[:, c_lat + MLA_ROPE:c_lat + MLA_ROPE + 3 * n_sb]
    hg = w[:, c_lat + MLA_ROPE + 3 * n_sb:c_lat + MLA_ROPE + 3 * n_sb + 4 * n_hg]
    gates = w[:, c_lat + MLA_ROPE + 3 * n_sb + 4 * n_hg:]
    copies = LANES // MLA_ROPE
    parts = ([w[:, :c_lat]] + [kr] * copies + [_rot_half(kr)] * copies
             + [sb[:, :n_sb] * (SB_HEAD_DIM ** -0.5 * LOG2E), sb[:, n_sb:]]
             + [hg[:, n_hg:2 * n_hg], hg[:, :n_hg], hg[:, 2 * n_hg:]]
             + [gates])
    o_ref[...] = jnp.concatenate(parts, axis=1).astype(o_ref.dtype)


def _inproj_weight(w, layer, width):
    _, k, n = w.shape
    return pl.pallas_call(
        _inproj_weight_kernel,
        grid=(k // CAST_ROWS,),
        in_specs=[pl.BlockSpec((1, CAST_ROWS, n), lambda r: (layer, r, 0))],
        out_specs=pl.BlockSpec((CAST_ROWS, width), lambda r: (r, 0)),
        out_shape=jax.ShapeDtypeStruct((k, width), BF16),
        compiler_params=_params(1),
        name="inproj_weight",
    )(w)


def _ffn_kernel(*refs, d_ff, fc, mix, embed, final):
    refs = list(refs)
    h_ref = refs.pop(0)
    if mix:
        ya_ref, yb_ref, yc_ref, gate_ref, wa_ref, wb_ref, wc_ref, wo_ref = refs[:8]
        del refs[:8]
    nw_ref, win_ref, wout_ref = refs[:3]
    del refs[:3]
    if embed:
        p_ref, pn_ref, wg_ref, wp_ref, fw_ref = refs[:5]
        del refs[:5]
    o_ref, xn_ref, hid_ref = refs

    x = h_ref[...]
    if mix:
        d = x.shape[1]
        merged = None
        for j, (y_ref, w_ref) in enumerate(((ya_ref, wa_ref), (yb_ref, wb_ref), (yc_ref, wc_ref))):
            gate = jax.nn.sigmoid(gate_ref[:, j * d:(j + 1) * d].astype(F32))
            term = gate * _dot(y_ref[...], w_ref[...])
            merged = term if merged is None else merged + term
        x = x + _dot(merged.astype(BF16), wo_ref[...])
    xn_ref[...] = _rms(x, nw_ref[...]).astype(BF16)
    for c in range(d_ff // fc):
        xn = xn_ref[...]
        g = _dot(xn, win_ref[:, c * fc:(c + 1) * fc])
        up = _dot(xn, win_ref[:, d_ff + c * fc:d_ff + (c + 1) * fc])
        hid_ref[:, c * fc:(c + 1) * fc] = (g * jax.nn.sigmoid(g) * up).astype(BF16)
    h = x + 0.5 * _dot(hid_ref[...], wout_ref[...])
    if embed:
        gate = jax.nn.sigmoid(_dot(_rms(h, pn_ref[...]).astype(BF16), wg_ref[...]))
        h = h + _dot(p_ref[0].astype(BF16), wp_ref[...]) * gate
        if final:
            h = _rms(h, fw_ref[...])
    o_ref[...] = h


def _ffn(h, nw, w_in, w_out, mix=None, embed=None, final=False):
    t, d = h.shape
    d_ff = w_out.shape[0]
    tm = min(ROW_BLOCK, t)
    assert t % tm == 0 and d_ff % FFN_CHUNK == 0
    row = lambda n: pl.BlockSpec((tm, n), lambda i: (i, 0))
    in_specs, args = [row(d)], [h]
    if mix is not None:
        in_specs += [row(a.shape[1]) for a in mix[:4]] + [_resident(w.shape) for w in mix[4:]]
        args += list(mix)
    in_specs += [_resident((1, d)), _resident((d, 2 * d_ff)), _resident((d_ff, d))]
    args += [nw, w_in, w_out]
    if embed is not None:
        p, layer, pn, wg, wp, fw = embed
        in_specs += [pl.BlockSpec((1, tm, p.shape[2]), lambda i: (layer, i, 0)),
                     _resident(pn.shape), _resident(wg.shape), _resident(wp.shape), _resident(fw.shape)]
        args += [p, pn, wg, wp, fw]
    return pl.pallas_call(
        functools.partial(_ffn_kernel, d_ff=d_ff, fc=FFN_CHUNK, mix=mix is not None, embed=embed is not None,
                          final=final),
        grid=(t // tm,),
        in_specs=in_specs,
        out_specs=row(d),
        out_shape=jax.ShapeDtypeStruct((t, d), F32),
        scratch_shapes=[pltpu.VMEM((tm, d), BF16), pltpu.VMEM((tm, d_ff), BF16)],
        compiler_params=_params(1),
        name="ffn",
    )(*args)


def _inproj_kernel(h_ref, nw_ref, w_ref, *o_refs, widths, nc):
    u = _rms(h_ref[...], nw_ref[...]).astype(BF16)
    start = 0
    for o_ref, width in zip(o_refs, widths):
        for c in range(0, width, nc):
            n = min(nc, width - c)
            o_ref[:, c:c + n] = _dot(u, w_ref[:, start + c:start + c + n]).astype(o_ref.dtype)
        start += width


def _inproj(h, nw, w, widths, dtypes):
    t, d = h.shape
    tm = min(ROW_BLOCK, t)
    assert t % tm == 0 and sum(widths) == w.shape[1] and all(n % LANES == 0 for n in widths)
    return pl.pallas_call(
        functools.partial(_inproj_kernel, widths=widths, nc=PROJ_CHUNK),
        grid=(t // tm,),
        in_specs=[pl.BlockSpec((tm, d), lambda i: (i, 0)), _resident((1, d)), _resident(w.shape)],
        out_specs=[pl.BlockSpec((tm, n), lambda i: (i, 0)) for n in widths],
        out_shape=[jax.ShapeDtypeStruct((t, n), dt) for n, dt in zip(widths, dtypes)],
        compiler_params=_params(1),
        name="inproj",
    )(h, nw, w)


def _rope_kernel(pos_ref, inv_ref, cos_ref, sin_ref):
    ang = pos_ref[0].astype(F32) * inv_ref[...]
    cos_ref[0] = jnp.cos(ang)
    sin_ref[0] = jnp.sin(ang)


def _rope_tables(pos, inv):
    b, seq, _ = pos.shape
    out = jax.ShapeDtypeStruct((b, seq, LANES), F32)
    spec = pl.BlockSpec((1, seq, LANES), lambda i: (i, 0, 0))
    return pl.pallas_call(
        _rope_kernel,
        grid=(b,),
        in_specs=[pl.BlockSpec((1, seq, 1), lambda i: (i, 0, 0)), _resident(inv.shape)],
        out_specs=[spec, spec],
        out_shape=[out, out],
        compiler_params=_params(1),
        name="rope_tables",
    )(pos, inv)


def _mla_kernel(x_ref, cos_ref, sin_ref, qn_ref, kvn_ref, wqn_ref, wqr_ref, wqrp_ref, wk_ref, wv_ref,
                o_ref, qcat_ref, kcat_ref, v_ref, *, seq, tq, scale):
    p = pl.program_id(1)
    n_pairs = MLA_HEADS // 2

    @pl.when(p == 0)
    def _project():
        rb = min(ROW_BLOCK, seq)
        for r in range(seq // rb):
            rows = slice(r * rb, (r + 1) * rb)
            x = x_ref[0, rows, :]
            c_q = x[:, :MLA_Q_LORA].astype(F32)
            c_kv = x[:, MLA_Q_LORA:MLA_Q_LORA + MLA_KV_LORA].astype(F32)
            kr = x[:, MLA_Q_LORA + MLA_KV_LORA:MLA_Q_LORA + MLA_KV_LORA + LANES].astype(F32)
            krp = x[:, MLA_Q_LORA + MLA_KV_LORA + LANES:].astype(F32)
            cos, sin = cos_ref[0, rows, :], sin_ref[0, rows, :]
            cqn = _rms(c_q, qn_ref[...]).astype(BF16)
            ckvn = _rms(c_kv, kvn_ref[...]).astype(BF16)
            k_rope = (kr * cos + krp * sin).astype(BF16)
            halves = lambda a: (a[:, :LANES], a[:, LANES:])
            ropes = zip(halves(_dot(cqn, wqr_ref[...])), halves(_dot(cqn, wqrp_ref[...])))
            for g, (qr, qrp) in enumerate(ropes):
                q_rope = ((qr * cos + qrp * sin) * scale).astype(BF16)
                qcat_ref[2 * g, rows, LANES:] = q_rope
                qcat_ref[2 * g + 1, rows, LANES:] = q_rope
            for g in range(n_pairs // 2):
                cols = slice(2 * g * LANES, (2 * g + 2) * LANES)
                parts = zip(halves(_dot(cqn, wqn_ref[:, cols])), halves(_dot(ckvn, wk_ref[:, cols])),
                            halves(_dot(ckvn, wv_ref[:, cols])))
                for pp, (qn, kn, vv) in enumerate(parts, start=2 * g):
                    qcat_ref[pp, rows, :LANES] = (qn * scale).astype(BF16)
                    kcat_ref[pp, rows, :LANES] = kn.astype(BF16)
                    kcat_ref[pp, rows, LANES:] = k_rope
                    v_ref[pp, rows, :] = vv.astype(BF16)

    lane = lax.broadcasted_iota(jnp.int32, (1, 2 * LANES), 1)
    out_lane = lax.broadcasted_iota(jnp.int32, (1, LANES), 1)
    ri = lax.broadcasted_iota(jnp.int32, (tq, tq), 0)
    ci = lax.broadcasted_iota(jnp.int32, (tq, tq), 1)
    diag_mask = (ci // CHUNK) <= (ri // CHUNK)
    group = jnp.where(lane < LANES, lane // MLA_NOPE, 2 + (lane - LANES) // MLA_ROPE)

    for n in range(seq // tq):
        keys = [slice(j * tq, (j + 1) * tq) for j in range(n + 1)]
        qc = qcat_ref[p, keys[n], :]
        s = []
        for hh in range(2):
            own = (group == hh) | (group == 2 + (p % 2) * 2 + hh)
            qh = qc * jnp.where(own, 1.0, 0.0).astype(BF16)
            sh = [lax.dot_general(qh, kcat_ref[p, kj, :], NT_DIMS, preferred_element_type=F32)
                  for kj in keys]
            sh[n] = jnp.where(diag_mask, sh[n], -jnp.inf)
            s.append(sh)
        m = [jnp.max(functools.reduce(jnp.maximum, sh), axis=-1, keepdims=True) for sh in s]
        outs = []
        for hh in range(2):
            pr = [jnp.exp2(sj - m[hh]) for sj in s[hh]]
            l = jnp.sum(functools.reduce(jnp.add, pr), axis=-1, keepdims=True)
            acc = None
            for j, kj in enumerate(keys):
                t = _dot(pr[j].astype(BF16), v_ref[p, kj, :])
                acc = t if acc is None else acc + t
            outs.append(acc * (1.0 / l))
        o_ref[0, keys[n], :] = jnp.where(out_lane < MLA_V, outs[0], outs[1]).astype(o_ref.dtype)


def _sb_kernel(q_ref, k_ref, v_ref, tri_ref, o_ref, *, tq, n_q):
    lane = lax.broadcasted_iota(jnp.int32, (1, LANES), 1)
    ri = lax.broadcasted_iota(jnp.int32, (tq, tq), 0)
    ci = lax.broadcasted_iota(jnp.int32, (tq, tq), 1)
    diag_mask = ci < ri
    sign = jnp.int32(-2 ** 31)
    rows = [slice(j * tq, (j + 1) * tq) for j in range(n_q)]

    def sweep(blocks, state):
        tiles = [(n, hh, j) for n, js in blocks.items() for j in js for hh in range(2)]
        qhs = {}
        for n in blocks:
            q = q_ref[0, rows[n], :]
            for hh in range(2):
                qhs[n, hh] = jnp.where((lane // SB_HEAD_DIM) == hh, q, jnp.zeros_like(q))
        z2 = {t: lax.dot_general(qhs[t[0], t[1]], k_ref[0, rows[t[2]], :], NT_DIMS,
                                 preferred_element_type=F32) for t in tiles}
        suffix, total = {}, {}
        for t in tiles:
            neg_abs = lax.bitcast_convert_type(lax.bitcast_convert_type(z2[t], jnp.int32) | sign, F32)
            sp2 = jnp.maximum(z2[t], 0.0) + jnp.log(1.0 + jnp.exp2(neg_abs)) * INV_LN2
            keep = jnp.where(diag_mask, sp2, 0.0) if t[2] == t[0] else sp2
            suffix[t] = _dot(keep.astype(BF16), tri_ref[...])
            total[t] = jnp.sum(keep, axis=-1, keepdims=True)
        state = dict(state)
        for n, hh, j in tiles:
            rest, acc = state[n, hh]
            a = jnp.exp2(jnp.minimum(z2[n, hh, j] - suffix[n, hh, j] - rest, 0.0))
            if j == n:
                a = jnp.where(diag_mask, a, 0.0)
            pv = _dot(a.astype(BF16), v_ref[0, rows[j], :])
            state[n, hh] = (rest + total[n, hh, j], pv if acc is None else acc + pv)
        return state

    def store(n, state):
        o_ref[0, rows[n], :] = jnp.where(lane < SB_HEAD_DIM, state[n, 0][1], state[n, 1][1]).astype(o_ref.dtype)

    start = {(n, hh): (jnp.zeros((tq, 1), F32), None) for n in range(n_q) for hh in range(2)}
    near = sweep({n: [j for j in (n, n - 1) if j >= 0] for n in range(n_q)}, start)
    for n in range(n_q):
        store(n, near)

    for n in range(2, n_q):
        alive = jnp.minimum(jnp.min(near[n, 0][0]), jnp.min(near[n, 1][0])) < SB_DEAD_LOG2

        @pl.when(alive)
        def _(n=n):
            far = sweep({n: list(range(n - 2, -1, -1))}, {k: v for k, v in near.items() if k[0] == n})
            store(n, far)


def _mla_sb_kernel(x_ref, cos_ref, sin_ref, qn_ref, kvn_ref, wqn_ref, wqr_ref, wqrp_ref, wk_ref, wv_ref,
                   sq_ref, sk_ref, sv_ref, tri_ref, oa_ref, ob_ref, qcat_ref, kcat_ref, v_ref, *, seq, tq, scale):
    _mla_kernel(x_ref, cos_ref, sin_ref, qn_ref, kvn_ref, wqn_ref, wqr_ref, wqrp_ref, wk_ref, wv_ref,
                oa_ref, qcat_ref, kcat_ref, v_ref, seq=seq, tq=tq, scale=scale)
    _sb_kernel(sq_ref, sk_ref, sv_ref, tri_ref, ob_ref, tq=tq, n_q=seq // tq)


def _mla_sb(x, cos, sin, qn, kvn, wqn, wqr, wqrp, wk, wv, qkv, tri):
    b, seq, _ = x.shape
    assert MLA_HEADS == SB_HEADS and MLA_V == SB_HEAD_DIM
    n_pairs = MLA_HEADS // 2
    tq = min(ATT_BLOCK, seq)
    assert seq % tq == 0 and seq % min(ROW_BLOCK, seq) == 0 and tq % CHUNK == 0
    scale = float((MLA_NOPE + MLA_ROPE) ** -0.5) * LOG2E
    col = lambda off: pl.BlockSpec((1, seq, LANES), lambda i, j: (i, 0, off + j))
    full = lambda n: pl.BlockSpec((1, seq, n), lambda i, j: (i, 0, 0))
    out = jax.ShapeDtypeStruct((b, seq, MLA_HEADS * MLA_V), BF16)
    return pl.pallas_call(
        functools.partial(_mla_sb_kernel, seq=seq, tq=tq, scale=scale),
        grid=(b, n_pairs),
        in_specs=[full(MLA_IN_WIDTH), full(LANES), full(LANES),
                  _resident(qn.shape), _resident(kvn.shape),
                  _resident(wqn.shape), _resident(wqr.shape), _resident(wqrp.shape),
                  _resident(wk.shape), _resident(wv.shape),
                  col(0), col(n_pairs), col(2 * n_pairs), _resident(tri.shape)],
        out_specs=[col(0), col(0)],
        out_shape=[out, out],
        scratch_shapes=[pltpu.VMEM((n_pairs, seq, 2 * LANES), BF16),
                        pltpu.VMEM((n_pairs, seq, 2 * LANES), BF16),
                        pltpu.VMEM((n_pairs, seq, LANES), BF16)],
        compiler_params=_params(2),
        name="mla_sb",
    )(x, cos, sin, qn, kvn, wqn, wqr, wqrp, wk, wv, qkv, qkv, qkv, tri)


def _hgrn_kernel(f_ref, qig_ref, lbp_ref, nw_ref, tri_ref, o_ref, b_ref, *, seq, layer, unroll):
    width = HG_HEADS * HG_KEY
    lbp = lbp_ref[...]
    e = jnp.exp(lbp - jnp.max(lbp, axis=0, keepdims=True))
    sm = e / jnp.sum(e, axis=0, keepdims=True)
    lb = jnp.zeros((1, width), F32)
    for j in range(1, layer + 1):
        lb = lb + sm[j:j + 1, :]
    lb = jnp.clip(lb, 0.0, 1.0 - 1e-6)
    lb_floor = jnp.maximum(lb, LB_FLOOR)

    gb = tri_ref.shape[0]
    for r in range(seq // gb):
        rows = slice(r * gb, (r + 1) * gb)
        log_f = jnp.minimum(jnp.log(lb_floor + (1.0 - lb) * jax.nn.sigmoid(f_ref[0, rows, :])), 0.0)
        acc = None
        for part in _split_bf16(log_f * LOG2E, 2):
            t = _dot(tri_ref[...], part)
            acc = t if acc is None else acc + t
        b_ref[rows, :] = acc

    n_sub = CHUNK // HG_SUB
    causal = (lax.broadcasted_iota(jnp.int32, (CHUNK, CHUNK), 1)
              <= lax.broadcasted_iota(jnp.int32, (CHUNK, CHUNK), 0))

    def step(n, states):
        states = list(states)
        tiles = [(u, h) for u in range(unroll) for h in range(HG_HEADS)]
        rows = {u: pl.ds(pl.multiple_of((n * unroll + u) * CHUNK, CHUNK), CHUNK) for u in range(unroll)}
        cols = {h: slice(h * HG_KEY, (h + 1) * HG_KEY) for h in range(HG_HEADS)}

        work = {}
        for u, h in tiles:
            x = f_ref[0, rows[u], cols[h]]
            b = b_ref[rows[u], cols[h]]
            q_raw = qig_ref[0, rows[u], cols[h]].astype(F32)
            q = q_raw * jax.nn.sigmoid(q_raw)
            k = (1.0 - lb[:, cols[h]]) * jax.nn.sigmoid(-x)
            v16 = qig_ref[0, rows[u], width + h * HG_KEY:width + (h + 1) * HG_KEY]
            b_last = b[CHUNK - 1:CHUNK, :]
            qd = (q * jnp.exp2(b)).astype(BF16)
            kd = (k * jnp.exp2(b_last - b)).astype(BF16)
            update = lax.dot_general(v16, kd, TN_DIMS, preferred_element_type=F32)
            qf, kf = [], []
            for i in range(n_sub):
                lo, hi = i * HG_SUB, (i + 1) * HG_SUB
                b_ref_i = jnp.zeros((1, HG_KEY), F32) if i == 0 else b[lo - 1:lo, :]
                qf_i = q[lo:hi, :] * jnp.exp2(b[lo:hi, :] - b_ref_i)
                above = [jnp.zeros((lo, HG_KEY), F32)] if lo else []
                below = [jnp.zeros((CHUNK - hi, HG_KEY), F32)] if hi < CHUNK else []
                qf.append(jnp.concatenate(above + [qf_i] + below, axis=0))
                kf.append(jnp.concatenate([k[:hi, :] * jnp.exp2(b_ref_i - b[:hi, :])] + below, axis=0))
            a = lax.dot_general(jnp.concatenate(qf, axis=1).astype(BF16),
                                jnp.concatenate(kf, axis=1).astype(BF16), NT_DIMS,
                                preferred_element_type=F32)
            a = jnp.where(causal, a, 0.0).astype(BF16)
            work[u, h] = (qd, update, jnp.exp2(b_last), a, v16)

        inter = {}
        for u, h in tiles:
            qd, update, decay, _, _ = work[u, h]
            inter[u, h] = lax.dot_general(qd, states[h].astype(BF16), NT_DIMS, preferred_element_type=F32)
            states[h] = states[h] * decay + update

        for u, h in tiles:
            _, _, _, a, v16 = work[u, h]
            o = inter[u, h] + _dot(a, v16)
            o = o * lax.rsqrt(jnp.mean(o * o, axis=-1, keepdims=True) + EPS) * nw_ref[:, cols[h]]
            g = qig_ref[0, rows[u], 2 * width + h * HG_KEY:2 * width + (h + 1) * HG_KEY].astype(F32)
            o_ref[0, rows[u], cols[h]] = (o * (g * jax.nn.sigmoid(g))).astype(o_ref.dtype)
        return tuple(states)

    zero = jnp.zeros((HG_KEY, HG_KEY), F32)
    lax.fori_loop(0, seq // (CHUNK * unroll), step, (zero,) * HG_HEADS)


def _hgrn(hg_f, hg_qig, lbp, nw, tri, layer):
    b, seq, width = hg_f.shape
    assert seq % (CHUNK * HG_UNROLL) == 0 and seq % tri.shape[0] == 0
    return pl.pallas_call(
        functools.partial(_hgrn_kernel, seq=seq, layer=layer, unroll=HG_UNROLL),
        grid=(b,),
        in_specs=[pl.BlockSpec((1, seq, width), lambda i: (i, 0, 0)),
                  pl.BlockSpec((1, seq, 3 * width), lambda i: (i, 0, 0)),
                  _resident(lbp.shape), _resident(nw.shape), _resident(tri.shape)],
        out_specs=pl.BlockSpec((1, seq, width), lambda i: (i, 0, 0)),
        out_shape=jax.ShapeDtypeStruct((b, seq, width), BF16),
        scratch_shapes=[pltpu.VMEM((seq, width), F32)],
        compiler_params=_params(1),
        name="hgrn2",
    )(hg_f, hg_qig, lbp, nw, tri)


def _lower(n):
    r = lax.broadcasted_iota(jnp.int32, (n, n), 0)
    c = lax.broadcasted_iota(jnp.int32, (n, n), 1)
    return r >= c


def kernel(x, p, positions, ffn_a_norm, ffn_a_w_in, ffn_a_w_out, mix_norm, w_in, mla_q_norm, mla_w_uq, mla_kv_norm, mla_w_ukv, hgrn_lower_bounds, hgrn_out_norm, w_br_mla, w_br_sb, w_br_hgrn, w_out, ffn_b_norm, ffn_b_w_in, ffn_b_w_out, ple_norm, w_ple_gate, w_ple_proj, final_norm):
    b, seq, d = x.shape
    depth = ffn_a_norm.shape[0]
    t = b * seq
    bf = lambda a: a.astype(BF16)
    row = lambda a: a.reshape(1, -1).astype(F32)

    tq = min(ATT_BLOCK, seq)
    sb_tri = _lower(tq).astype(BF16)
    gb = 4 * CHUNK
    r = lax.broadcasted_iota(jnp.int32, (gb, gb), 0)
    c = lax.broadcasted_iota(jnp.int32, (gb, gb), 1)
    hg_tri = ((c <= r) & (c // CHUNK == r // CHUNK)).astype(BF16)

    half = MLA_ROPE // 2
    inv = ROPE_BASE ** (-jnp.arange(half, dtype=F32) / half)
    inv = jnp.tile(inv, LANES // half).reshape(1, LANES)
    cos, sin = _rope_tables(positions.reshape(b, seq, 1), inv)

    hg_w = HG_HEADS * HG_KEY
    widths = (MLA_IN_WIDTH, 3 * SB_HEADS * SB_HEAD_DIM, hg_w, 3 * hg_w, N_BRANCH * d)

    h = x.reshape(t, d)
    for i in range(depth):
        h = _ffn(h, row(ffn_a_norm[i]), _layer_bf16(ffn_a_w_in, i), _layer_bf16(ffn_a_w_out, i))

        w_cat = _inproj_weight(w_in, i, sum(widths))
        mla_in, sb_in, hg_f, hg_qig, gates = _inproj(h, row(mix_norm[i]), w_cat, widths,
                                                     (BF16, BF16, F32, BF16, BF16))

        wq = mla_w_uq[i].reshape(MLA_Q_LORA, MLA_HEADS, MLA_NOPE + MLA_ROPE)
        wq_nope = bf(wq[:, :, :MLA_NOPE].reshape(MLA_Q_LORA, -1))
        wq_rope = wq[:, :, MLA_NOPE:]
        wkv = mla_w_ukv[i].reshape(MLA_KV_LORA, MLA_HEADS, MLA_NOPE + MLA_V)
        y_a, y_b = _mla_sb(
            mla_in.reshape(b, seq, -1), cos, sin, row(mla_q_norm[i]), row(mla_kv_norm[i]),
            wq_nope, bf(wq_rope.reshape(MLA_Q_LORA, -1)), bf(_rot_half(wq_rope).reshape(MLA_Q_LORA, -1)),
            bf(wkv[:, :, :MLA_NOPE].reshape(MLA_KV_LORA, -1)), bf(wkv[:, :, MLA_NOPE:].reshape(MLA_KV_LORA, -1)),
            sb_in.reshape(b, seq, -1), sb_tri)
        y_c = _hgrn(hg_f.reshape(b, seq, -1), hg_qig.reshape(b, seq, -1), hgrn_lower_bounds.astype(F32),
                    row(hgrn_out_norm[i]), hg_tri, i)

        h = _ffn(h, row(ffn_b_norm[i]), _layer_bf16(ffn_b_w_in, i), _layer_bf16(ffn_b_w_out, i),
                 mix=(y_a.reshape(t, -1), y_b.reshape(t, -1), y_c.reshape(t, -1), gates, _layer_bf16(w_br_mla, i),
                      _layer_bf16(w_br_sb, i), _layer_bf16(w_br_hgrn, i), _layer_bf16(w_out, i)),
                 embed=(p.reshape(depth, t, -1), i, row(ple_norm[i]), _layer_bf16(w_ple_gate, i),
                        _layer_bf16(w_ple_proj, i), row(final_norm)),
                 final=(i == depth - 1))
    return h.reshape(b, seq, d)
```

```python
import functools

import jax
import jax.numpy as jnp
from jax import lax
from jax.experimental import pallas as pl
from jax.experimental.pallas import tpu as pltpu

F32 = jnp.float32
BF16 = jnp.bfloat16

EPS = 1e-6
LB_FLOOR = 1e-30
CHUNK = 64
HG_SUB = 16
MLA_HEADS = 8
MLA_NOPE = 64
MLA_ROPE = 32
MLA_V = 64
MLA_Q_LORA = 384
MLA_KV_LORA = 256
ROPE_BASE = 10000.0
LOG2E = 1.4426950408889634
INV_LN2 = LOG2E
SB_DEAD_LOG2 = 160.0
SB_HEADS = 8
SB_HEAD_DIM = 64
HG_HEADS = 4
HG_KEY = 128
N_BRANCH = 3

LANES = 128
MLA_IN_WIDTH = MLA_Q_LORA + MLA_KV_LORA + 2 * LANES
MXU_WIDTH = 256
ATT_BLOCK = MXU_WIDTH
FFN_CHUNK = MXU_WIDTH
PROJ_CHUNK = MXU_WIDTH
ROW_BLOCK = 512
CAST_ROWS = 256
HG_UNROLL = 32
VMEM_LIMIT_BYTES = 56 * 1024 * 1024

NT_DIMS = (((1,), (1,)), ((), ()))
TN_DIMS = (((0,), (0,)), ((), ()))


def _rms(x, w):
    return x * lax.rsqrt(jnp.mean(x * x, axis=-1, keepdims=True) + EPS) * w


def _dot(a, b):
    return jnp.dot(a, b, preferred_element_type=F32)


def _split_bf16(x, terms):
    parts = []
    for _ in range(terms - 1):
        hi = x.astype(BF16)
        parts.append(hi)
        x = x - hi.astype(F32)
    parts.append(x.astype(BF16))
    return parts


def _rot_half(w):
    half = MLA_ROPE // 2
    return jnp.concatenate([-w[..., half:], w[..., :half]], axis=-1)


def _params(n_axes):
    return pltpu.CompilerParams(dimension_semantics=("arbitrary",) * n_axes,
                                vmem_limit_bytes=VMEM_LIMIT_BYTES)


def _resident(shape):
    return pl.BlockSpec(shape, lambda *_: (0,) * len(shape), pipeline_mode=pl.Buffered(1))


def _cast_kernel(w_ref, o_ref):
    o_ref[...] = w_ref[0].astype(o_ref.dtype)


def _layer_bf16(w, layer):
    _, k, n = w.shape
    tk = CAST_ROWS if k % CAST_ROWS == 0 else k
    return pl.pallas_call(
        _cast_kernel,
        grid=(k // tk,),
        in_specs=[pl.BlockSpec((1, tk, n), lambda r: (layer, r, 0))],
        out_specs=pl.BlockSpec((tk, n), lambda r: (r, 0)),
        out_shape=jax.ShapeDtypeStruct((k, n), BF16),
        compiler_params=_params(1),
        name="to_bf16",
    )(w)


def _inproj_weight_kernel(w_ref, o_ref):
    w = w_ref[0]
    c_lat = MLA_Q_LORA + MLA_KV_LORA
    n_sb, n_hg = SB_HEADS * SB_HEAD_DIM, HG_HEADS * HG_KEY
    kr = w[:, c_lat:c_lat + MLA_ROPE]
    sb = w[:, c_lat + MLA_ROPE:c_lat + MLA_ROPE + 3 * n_sb]
    hg = w[:, c_lat + MLA_ROPE + 3 * n_sb:c_lat + MLA_ROPE + 3 * n_sb + 4 * n_hg]
    gates = w[:, c_lat + MLA_ROPE + 3 * n_sb + 4 * n_hg:]
    copies = LANES // MLA_ROPE
    parts = ([w[:, :c_lat]] + [kr] * copies + [_rot_half(kr)] * copies
             + [sb[:, :n_sb] * (SB_HEAD_DIM ** -0.5 * LOG2E), sb[:, n_sb:]]
             + [hg[:, n_hg:2 * n_hg], hg[:, :n_hg], hg[:, 2 * n_hg:]]
             + [gates])
    o_ref[...] = jnp.concatenate(parts, axis=1).astype(o_ref.dtype)


def _inproj_weight(w, layer, width):
    _, k, n = w.shape
    return pl.pallas_call(
        _inproj_weight_kernel,
        grid=(k // CAST_ROWS,),
        in_specs=[pl.BlockSpec((1, CAST_ROWS, n), lambda r: (layer, r, 0))],
        out_specs=pl.BlockSpec((CAST_ROWS, width), lambda r: (r, 0)),
        out_shape=jax.ShapeDtypeStruct((k, width), BF16),
        compiler_params=_params(1),
        name="inproj_weight",
    )(w)


def _ffn_kernel(*refs, d_ff, fc, mix, embed, final):
    refs = list(refs)
    h_ref = refs.pop(0)
    if mix:
        ya_ref, yb_ref, yc_ref, gate_ref, wa_ref, wb_ref, wc_ref, wo_ref = refs[:8]
        del refs[:8]
    nw_ref, win_ref, wout_ref = refs[:3]
    del refs[:3]
    if embed:
        p_ref, pn_ref, wg_ref, wp_ref, fw_ref = refs[:5]
        del refs[:5]
    o_ref, xn_ref, hid_ref = refs

    x = h_ref[...]
    if mix:
        d = x.shape[1]
        merged = None
        for j, (y_ref, w_ref) in enumerate(((ya_ref, wa_ref), (yb_ref, wb_ref), (yc_ref, wc_ref))):
            gate = jax.nn.sigmoid(gate_ref[:, j * d:(j + 1) * d].astype(F32))
            term = gate * _dot(y_ref[...], w_ref[...])
            merged = term if merged is None else merged + term
        x = x + _dot(merged.astype(BF16), wo_ref[...])
    xn_ref[...] = _rms(x, nw_ref[...]).astype(BF16)
    for c in range(d_ff // fc):
        xn = xn_ref[...]
        g = _dot(xn, win_ref[:, c * fc:(c + 1) * fc])
        up = _dot(xn, win_ref[:, d_ff + c * fc:d_ff + (c + 1) * fc])
        hid_ref[:, c * fc:(c + 1) * fc] = (g * jax.nn.sigmoid(g) * up).astype(BF16)
    h = x + 0.5 * _dot(hid_ref[...], wout_ref[...])
    if embed:
        gate = jax.nn.sigmoid(_dot(_rms(h, pn_ref[...]).astype(BF16), wg_ref[...]))
        h = h + _dot(p_ref[0].astype(BF16), wp_ref[...]) * gate
        if final:
            h = _rms(h, fw_ref[...])
    o_ref[...] = h


def _ffn(h, nw, w_in, w_out, mix=None, embed=None, final=False):
    t, d = h.shape
    d_ff = w_out.shape[0]
    tm = min(ROW_BLOCK, t)
    assert t % tm == 0 and d_ff % FFN_CHUNK == 0
    row = lambda n: pl.BlockSpec((tm, n), lambda i: (i, 0))
    in_specs, args = [row(d)], [h]
    if mix is not None:
        in_specs += [row(a.shape[1]) for a in mix[:4]] + [_resident(w.shape) for w in mix[4:]]
        args += list(mix)
    in_specs += [_resident((1, d)), _resident((d, 2 * d_ff)), _resident((d_ff, d))]
    args += [nw, w_in, w_out]
    if embed is not None:
        p, layer, pn, wg, wp, fw = embed
        in_specs += [pl.BlockSpec((1, tm, p.shape[2]), lambda i: (layer, i, 0)),
                     _resident(pn.shape), _resident(wg.shape), _resident(wp.shape), _resident(fw.shape)]
        args += [p, pn, wg, wp, fw]
    return pl.pallas_call(
        functools.partial(_ffn_kernel, d_ff=d_ff, fc=FFN_CHUNK, mix=mix is not None, embed=embed is not None,
                          final=final),
        grid=(t // tm,),
        in_specs=in_specs,
        out_specs=row(d),
        out_shape=jax.ShapeDtypeStruct((t, d), F32),
        scratch_shapes=[pltpu.VMEM((tm, d), BF16), pltpu.VMEM((tm, d_ff), BF16)],
        compiler_params=_params(1),
        name="ffn",
    )(*args)


def _inproj_kernel(h_ref, nw_ref, w_ref, *o_refs, widths, nc):
    u = _rms(h_ref[...], nw_ref[...]).astype(BF16)
    start = 0
    for o_ref, width in zip(o_refs, widths):
        for c in range(0, width, nc):
            n = min(nc, width - c)
            o_ref[:, c:c + n] = _dot(u, w_ref[:, start + c:start + c + n]).astype(o_ref.dtype)
        start += width


def _inproj(h, nw, w, widths, dtypes):
    t, d = h.shape
    tm = min(ROW_BLOCK, t)
    assert t % tm == 0 and sum(widths) == w.shape[1] and all(n % LANES == 0 for n in widths)
    return pl.pallas_call(
        functools.partial(_inproj_kernel, widths=widths, nc=PROJ_CHUNK),
        grid=(t // tm,),
        in_specs=[pl.BlockSpec((tm, d), lambda i: (i, 0)), _resident((1, d)), _resident(w.shape)],
        out_specs=[pl.BlockSpec((tm, n), lambda i: (i, 0)) for n in widths],
        out_shape=[jax.ShapeDtypeStruct((t, n), dt) for n, dt in zip(widths, dtypes)],
        compiler_params=_params(1),
        name="inproj",
    )(h, nw, w)


def _rope_kernel(pos_ref, inv_ref, cos_ref, sin_ref):
    ang = pos_ref[0].astype(F32) * inv_ref[...]
    cos_ref[0] = jnp.cos(ang)
    sin_ref[0] = jnp.sin(ang)


def _rope_tables(pos, inv):
    b, seq, _ = pos.shape
    out = jax.ShapeDtypeStruct((b, seq, LANES), F32)
    spec = pl.BlockSpec((1, seq, LANES), lambda i: (i, 0, 0))
    return pl.pallas_call(
        _rope_kernel,
        grid=(b,),
        in_specs=[pl.BlockSpec((1, seq, 1), lambda i: (i, 0, 0)), _resident(inv.shape)],
        out_specs=[spec, spec],
        out_shape=[out, out],
        compiler_params=_params(1),
        name="rope_tables",
    )(pos, inv)


def _mla_kernel(x_ref, cos_ref, sin_ref, qn_ref, kvn_ref, wqn_ref, wqr_ref, wqrp_ref, wk_ref, wv_ref,
                o_ref, qcat_ref, kcat_ref, v_ref, *, seq, tq, scale):
    p = pl.program_id(1)
    n_pairs = MLA_HEADS // 2

    @pl.when(p == 0)
    def _project():
        rb = min(ROW_BLOCK, seq)
        for r in range(seq // rb):
            rows = slice(r * rb, (r + 1) * rb)
            x = x_ref[0, rows, :]
            c_q = x[:, :MLA_Q_LORA].astype(F32)
            c_kv = x[:, MLA_Q_LORA:MLA_Q_LORA + MLA_KV_LORA].astype(F32)
            kr = x[:, MLA_Q_LORA + MLA_KV_LORA:MLA_Q_LORA + MLA_KV_LORA + LANES].astype(F32)
            krp = x[:, MLA_Q_LORA + MLA_KV_LORA + LANES:].astype(F32)
            cos, sin = cos_ref[0, rows, :], sin_ref[0, rows, :]
            cqn = _rms(c_q, qn_ref[...]).astype(BF16)
            ckvn = _rms(c_kv, kvn_ref[...]).astype(BF16)
            k_rope = (kr * cos + krp * sin).astype(BF16)
            halves = lambda a: (a[:, :LANES], a[:, LANES:])
            ropes = zip(halves(_dot(cqn, wqr_ref[...])), halves(_dot(cqn, wqrp_ref[...])))
            for g, (qr, qrp) in enumerate(ropes):
                q_rope = ((qr * cos + qrp * sin) * scale).astype(BF16)
                qcat_ref[2 * g, rows, LANES:] = q_rope
                qcat_ref[2 * g + 1, rows, LANES:] = q_rope
            for g in range(n_pairs // 2):
                cols = slice(2 * g * LANES, (2 * g + 2) * LANES)
                parts = zip(halves(_dot(cqn, wqn_ref[:, cols])), halves(_dot(ckvn, wk_ref[:, cols])),
                            halves(_dot(ckvn, wv_ref[:, cols])))
                for pp, (qn, kn, vv) in enumerate(parts, start=2 * g):
                    qcat_ref[pp, rows, :LANES] = (qn * scale).astype(BF16)
                    kcat_ref[pp, rows, :LANES] = kn.astype(BF16)
                    kcat_ref[pp, rows, LANES:] = k_rope
                    v_ref[pp, rows, :] = vv.astype(BF16)

    lane = lax.broadcasted_iota(jnp.int32, (1, 2 * LANES), 1)
    out_lane = lax.broadcasted_iota(jnp.int32, (1, LANES), 1)
    ri = lax.broadcasted_iota(jnp.int32, (tq, tq), 0)
    ci = lax.broadcasted_iota(jnp.int32, (tq, tq), 1)
    diag_mask = (ci // CHUNK) <= (ri // CHUNK)
    group = jnp.where(lane < LANES, lane // MLA_NOPE, 2 + (lane - LANES) // MLA_ROPE)

    for n in range(seq // tq):
        keys = [slice(j * tq, (j + 1) * tq) for j in range(n + 1)]
        qc = qcat_ref[p, keys[n], :]
        s = []
        for hh in range(2):
            own = (group == hh) | (group == 2 + (p % 2) * 2 + hh)
            qh = qc * jnp.where(own, 1.0, 0.0).astype(BF16)
            sh = [lax.dot_general(qh, kcat_ref[p, kj, :], NT_DIMS, preferred_element_type=F32)
                  for kj in keys]
            sh[n] = jnp.where(diag_mask, sh[n], -jnp.inf)
            s.append(sh)
        m = [jnp.max(functools.reduce(jnp.maximum, sh), axis=-1, keepdims=True) for sh in s]
        outs = []
        for hh in range(2):
            pr = [jnp.exp2(sj - m[hh]) for sj in s[hh]]
            l = jnp.sum(functools.reduce(jnp.add, pr), axis=-1, keepdims=True)
            acc = None
            for j, kj in enumerate(keys):
                t = _dot(pr[j].astype(BF16), v_ref[p, kj, :])
                acc = t if acc is None else acc + t
            outs.append(acc * (1.0 / l))
        o_ref[0, keys[n], :] = jnp.where(out_lane < MLA_V, outs[0], outs[1]).astype(o_ref.dtype)


def _sb_kernel(q_ref, k_ref, v_ref, tri_ref, o_ref, *, tq, n_q):
    lane = lax.broadcasted_iota(jnp.int32, (1, LANES), 1)
    ri = lax.broadcasted_iota(jnp.int32, (tq, tq), 0)
    ci = lax.broadcasted_iota(jnp.int32, (tq, tq), 1)
    diag_mask = ci < ri
    sign = jnp.int32(-2 ** 31)
    rows = [slice(j * tq, (j + 1) * tq) for j in range(n_q)]

    def sweep(blocks, state):
        tiles = [(n, hh, j) for n, js in blocks.items() for j in js for hh in range(2)]
        qhs = {}
        for n in blocks:
            q = q_ref[0, rows[n], :]
            for hh in range(2):
                qhs[n, hh] = jnp.where((lane // SB_HEAD_DIM) == hh, q, jnp.zeros_like(q))
        z2 = {t: lax.dot_general(qhs[t[0], t[1]], k_ref[0, rows[t[2]], :], NT_DIMS,
                                 preferred_element_type=F32) for t in tiles}
        suffix, total = {}, {}
        for t in tiles:
            neg_abs = lax.bitcast_convert_type(lax.bitcast_convert_type(z2[t], jnp.int32) | sign, F32)
            sp2 = jnp.maximum(z2[t], 0.0) + jnp.log(1.0 + jnp.exp2(neg_abs)) * INV_LN2
            keep = jnp.where(diag_mask, sp2, 0.0) if t[2] == t[0] else sp2
            suffix[t] = _dot(keep.astype(BF16), tri_ref[...])
            total[t] = jnp.sum(keep, axis=-1, keepdims=True)
        state = dict(state)
        for n, hh, j in tiles:
            rest, acc = state[n, hh]
            a = jnp.exp2(jnp.minimum(z2[n, hh, j] - suffix[n, hh, j] - rest, 0.0))
            if j == n:
                a = jnp.where(diag_mask, a, 0.0)
            pv = _dot(a.astype(BF16), v_ref[0, rows[j], :])
            state[n, hh] = (rest + total[n, hh, j], pv if acc is None else acc + pv)
        return state

    def store(n, state):
        o_ref[0, rows[n], :] = jnp.where(lane < SB_HEAD_DIM, state[n, 0][1], state[n, 1][1]).astype(o_ref.dtype)

    start = {(n, hh): (jnp.zeros((tq, 1), F32), None) for n in range(n_q) for hh in range(2)}
    near = sweep({n: [j for j in (n, n - 1) if j >= 0] for n in range(n_q)}, start)
    for n in range(n_q):
        store(n, near)

    for n in range(2, n_q):
        alive = jnp.minimum(jnp.min(near[n, 0][0]), jnp.min(near[n, 1][0])) < SB_DEAD_LOG2

        @pl.when(alive)
        def _(n=n):
            far = sweep({n: list(range(n - 2, -1, -1))}, {k: v for k, v in near.items() if k[0] == n})
            store(n, far)


def _mla_sb_kernel(x_ref, cos_ref, sin_ref, qn_ref, kvn_ref, wqn_ref, wqr_ref, wqrp_ref, wk_ref, wv_ref,
                   sq_ref, sk_ref, sv_ref, tri_ref, oa_ref, ob_ref, qcat_ref, kcat_ref, v_ref, *, seq, tq, scale):
    _mla_kernel(x_ref, cos_ref, sin_ref, qn_ref, kvn_ref, wqn_ref, wqr_ref, wqrp_ref, wk_ref, wv_ref,
                oa_ref, qcat_ref, kcat_ref, v_ref, seq=seq, tq=tq, scale=scale)
    _sb_kernel(sq_ref, sk_ref, sv_ref, tri_ref, ob_ref, tq=tq, n_q=seq // tq)


def _mla_sb(x, cos, sin, qn, kvn, wqn, wqr, wqrp, wk, wv, qkv, tri):
    b, seq, _ = x.shape
    assert MLA_HEADS == SB_HEADS and MLA_V == SB_HEAD_DIM
    n_pairs = MLA_HEADS // 2
    tq = min(ATT_BLOCK, seq)
    assert seq % tq == 0 and seq % min(ROW_BLOCK, seq) == 0 and tq % CHUNK == 0
    scale = float((MLA_NOPE + MLA_ROPE) ** -0.5) * LOG2E
    col = lambda off: pl.BlockSpec((1, seq, LANES), lambda i, j: (i, 0, off + j))
    full = lambda n: pl.BlockSpec((1, seq, n), lambda i, j: (i, 0, 0))
    out = jax.ShapeDtypeStruct((b, seq, MLA_HEADS * MLA_V), BF16)
    return pl.pallas_call(
        functools.partial(_mla_sb_kernel, seq=seq, tq=tq, scale=scale),
        grid=(b, n_pairs),
        in_specs=[full(MLA_IN_WIDTH), full(LANES), full(LANES),
                  _resident(qn.shape), _resident(kvn.shape),
                  _resident(wqn.shape), _resident(wqr.shape), _resident(wqrp.shape),
                  _resident(wk.shape), _resident(wv.shape),
                  col(0), col(n_pairs), col(2 * n_pairs), _resident(tri.shape)],
        out_specs=[col(0), col(0)],
        out_shape=[out, out],
        scratch_shapes=[pltpu.VMEM((n_pairs, seq, 2 * LANES), BF16),
                        pltpu.VMEM((n_pairs, seq, 2 * LANES), BF16),
                        pltpu.VMEM((n_pairs, seq, LANES), BF16)],
        compiler_params=_params(2),
        name="mla_sb",
    )(x, cos, sin, qn, kvn, wqn, wqr, wqrp, wk, wv, qkv, qkv, qkv, tri)


def _hgrn_kernel(f_ref, qig_ref, lbp_ref, nw_ref, tri_ref, o_ref, b_ref, *, seq, layer, unroll):
    width = HG_HEADS * HG_KEY
    lbp = lbp_ref[...]
    e = jnp.exp(lbp - jnp.max(lbp, axis=0, keepdims=True))
    sm = e / jnp.sum(e, axis=0, keepdims=True)
    lb = jnp.zeros((1, width), F32)
    for j in range(1, layer + 1):
        lb = lb + sm[j:j + 1, :]
    lb = jnp.clip(lb, 0.0, 1.0 - 1e-6)
    lb_floor = jnp.maximum(lb, LB_FLOOR)

    gb = tri_ref.shape[0]
    for r in range(seq // gb):
        rows = slice(r * gb, (r + 1) * gb)
        log_f = jnp.minimum(jnp.log(lb_floor + (1.0 - lb) * jax.nn.sigmoid(f_ref[0, rows, :])), 0.0)
        acc = None
        for part in _split_bf16(log_f * LOG2E, 2):
            t = _dot(tri_ref[...], part)
            acc = t if acc is None else acc + t
        b_ref[rows, :] = acc

    n_sub = CHUNK // HG_SUB
    causal = (lax.broadcasted_iota(jnp.int32, (CHUNK, CHUNK), 1)
              <= lax.broadcasted_iota(jnp.int32, (CHUNK, CHUNK), 0))

    def step(n, states):
        states = list(states)
        tiles = [(u, h) for u in range(unroll) for h in range(HG_HEADS)]
        rows = {u: pl.ds(pl.multiple_of((n * unroll + u) * CHUNK, CHUNK), CHUNK) for u in range(unroll)}
        cols = {h: slice(h * HG_KEY, (h + 1) * HG_KEY) for h in range(HG_HEADS)}

        work = {}
        for u, h in tiles:
            x = f_ref[0, rows[u], cols[h]]
            b = b_ref[rows[u], cols[h]]
            q_raw = qig_ref[0, rows[u], cols[h]].astype(F32)
            q = q_raw * jax.nn.sigmoid(q_raw)
            k = (1.0 - lb[:, cols[h]]) * jax.nn.sigmoid(-x)
            v16 = qig_ref[0, rows[u], width + h * HG_KEY:width + (h + 1) * HG_KEY]
            b_last = b[CHUNK - 1:CHUNK, :]
            qd = (q * jnp.exp2(b)).astype(BF16)
            kd = (k * jnp.exp2(b_last - b)).astype(BF16)
            update = lax.dot_general(v16, kd, TN_DIMS, preferred_element_type=F32)
            qf, kf = [], []
            for i in range(n_sub):
                lo, hi = i * HG_SUB, (i + 1) * HG_SUB
                b_ref_i = jnp.zeros((1, HG_KEY), F32) if i == 0 else b[lo - 1:lo, :]
                qf_i = q[lo:hi, :] * jnp.exp2(b[lo:hi, :] - b_ref_i)
                above = [jnp.zeros((lo, HG_KEY), F32)] if lo else []
                below = [jnp.zeros((CHUNK - hi, HG_KEY), F32)] if hi < CHUNK else []
                qf.append(jnp.concatenate(above + [qf_i] + below, axis=0))
                kf.append(jnp.concatenate([k[:hi, :] * jnp.exp2(b_ref_i - b[:hi, :])] + below, axis=0))
            a = lax.dot_general(jnp.concatenate(qf, axis=1).astype(BF16),
                                jnp.concatenate(kf, axis=1).astype(BF16), NT_DIMS,
                                preferred_element_type=F32)
            a = jnp.where(causal, a, 0.0).astype(BF16)
            work[u, h] = (qd, update, jnp.exp2(b_last), a, v16)

        inter = {}
        for u, h in tiles:
            qd, update, decay, _, _ = work[u, h]
            inter[u, h] = lax.dot_general(qd, states[h].astype(BF16), NT_DIMS, preferred_element_type=F32)
            states[h] = states[h] * decay + update

        for u, h in tiles:
            _, _, _, a, v16 = work[u, h]
            o = inter[u, h] + _dot(a, v16)
            o = o * lax.rsqrt(jnp.mean(o * o, axis=-1, keepdims=True) + EPS) * nw_ref[:, cols[h]]
            g = qig_ref[0, rows[u], 2 * width + h * HG_KEY:2 * width + (h + 1) * HG_KEY].astype(F32)
            o_ref[0, rows[u], cols[h]] = (o * (g * jax.nn.sigmoid(g))).astype(o_ref.dtype)
        return tuple(states)

    zero = jnp.zeros((HG_KEY, HG_KEY), F32)
    lax.fori_loop(0, seq // (CHUNK * unroll), step, (zero,) * HG_HEADS)


def _hgrn(hg_f, hg_qig, lbp, nw, tri, layer):
    b, seq, width = hg_f.shape
    unroll = min(HG_UNROLL, seq // CHUNK)
    assert seq % (CHUNK * unroll) == 0 and seq % tri.shape[0] == 0
    return pl.pallas_call(
        functools.partial(_hgrn_kernel, seq=seq, layer=layer, unroll=unroll),
        grid=(b,),
        in_specs=[pl.BlockSpec((1, seq, width), lambda i: (i, 0, 0)),
                  pl.BlockSpec((1, seq, 3 * width), lambda i: (i, 0, 0)),
                  _resident(lbp.shape), _resident(nw.shape), _resident(tri.shape)],
        out_specs=pl.BlockSpec((1, seq, width), lambda i: (i, 0, 0)),
        out_shape=jax.ShapeDtypeStruct((b, seq, width), BF16),
        scratch_shapes=[pltpu.VMEM((seq, width), F32)],
        compiler_params=_params(1),
        name="hgrn2",
    )(hg_f, hg_qig, lbp, nw, tri)


def _lower(n):
    r = lax.broadcasted_iota(jnp.int32, (n, n), 0)
    c = lax.broadcasted_iota(jnp.int32, (n, n), 1)
    return r >= c


def kernel(x, p, positions, ffn_a_norm, ffn_a_w_in, ffn_a_w_out, mix_norm, w_in, mla_q_norm, mla_w_uq, mla_kv_norm, mla_w_ukv, hgrn_lower_bounds, hgrn_out_norm, w_br_mla, w_br_sb, w_br_hgrn, w_out, ffn_b_norm, ffn_b_w_in, ffn_b_w_out, ple_norm, w_ple_gate, w_ple_proj, final_norm):
    b, seq, d = x.shape
    depth = ffn_a_norm.shape[0]
    t = b * seq
    bf = lambda a: a.astype(BF16)
    row = lambda a: a.reshape(1, -1).astype(F32)

    tq = min(ATT_BLOCK, seq)
    sb_tri = _lower(tq).astype(BF16)
    gb = 4 * CHUNK
    r = lax.broadcasted_iota(jnp.int32, (gb, gb), 0)
    c = lax.broadcasted_iota(jnp.int32, (gb, gb), 1)
    hg_tri = ((c <= r) & (c // CHUNK == r // CHUNK)).astype(BF16)

    half = MLA_ROPE // 2
    inv = ROPE_BASE ** (-jnp.arange(half, dtype=F32) / half)
    inv = jnp.tile(inv, LANES // half).reshape(1, LANES)
    cos, sin = _rope_tables(positions.reshape(b, seq, 1), inv)

    hg_w = HG_HEADS * HG_KEY
    widths = (MLA_IN_WIDTH, 3 * SB_HEADS * SB_HEAD_DIM, hg_w, 3 * hg_w, N_BRANCH * d)

    h = x.reshape(t, d)
    for i in range(depth):
        h = _ffn(h, row(ffn_a_norm[i]), _layer_bf16(ffn_a_w_in, i), _layer_bf16(ffn_a_w_out, i))

        w_cat = _inproj_weight(w_in, i, sum(widths))
        mla_in, sb_in, hg_f, hg_qig, gates = _inproj(h, row(mix_norm[i]), w_cat, widths,
                                                     (BF16, BF16, F32, BF16, BF16))

        wq = mla_w_uq[i].reshape(MLA_Q_LORA, MLA_HEADS, MLA_NOPE + MLA_ROPE)
        wq_nope = bf(wq[:, :, :MLA_NOPE].reshape(MLA_Q_LORA, -1))
        wq_rope = wq[:, :, MLA_NOPE:]
        wkv = mla_w_ukv[i].reshape(MLA_KV_LORA, MLA_HEADS, MLA_NOPE + MLA_V)
        y_a, y_b = _mla_sb(
            mla_in.reshape(b, seq, -1), cos, sin, row(mla_q_norm[i]), row(mla_kv_norm[i]),
            wq_nope, bf(wq_rope.reshape(MLA_Q_LORA, -1)), bf(_rot_half(wq_rope).reshape(MLA_Q_LORA, -1)),
            bf(wkv[:, :, :MLA_NOPE].reshape(MLA_KV_LORA, -1)), bf(wkv[:, :, MLA_NOPE:].reshape(MLA_KV_LORA, -1)),
            sb_in.reshape(b, seq, -1), sb_tri)
        y_c = _hgrn(hg_f.reshape(b, seq, -1), hg_qig.reshape(b, seq, -1), hgrn_lower_bounds.astype(F32),
                    row(hgrn_out_norm[i]), hg_tri, i)

        h = _ffn(h, row(ffn_b_norm[i]), _layer_bf16(ffn_b_w_in, i), _layer_bf16(ffn_b_w_out, i),
                 mix=(y_a.reshape(t, -1), y_b.reshape(t, -1), y_c.reshape(t, -1), gates, _layer_bf16(w_br_mla, i),
                      _layer_bf16(w_br_sb, i), _layer_bf16(w_br_hgrn, i), _layer_bf16(w_out, i)),
                 embed=(p.reshape(depth, t, -1), i, row(ple_norm[i]), _layer_bf16(w_ple_gate, i),
                        _layer_bf16(w_ple_proj, i), row(final_norm)),
                 final=(i == depth - 1))
    return h.reshape(b, seq, d)
```

```python
import functools

import jax
import jax.numpy as jnp
from jax import lax
from jax.experimental import pallas as pl
from jax.experimental.pallas import tpu as pltpu

F32 = jnp.float32
BF16 = jnp.bfloat16

EPS = 1e-6
LB_FLOOR = 1e-30
CHUNK = 64
HG_SUB = 16
MLA_HEADS = 8
MLA_NOPE = 64
MLA_ROPE = 32
MLA_V = 64
MLA_Q_LORA = 384
MLA_KV_LORA = 256
ROPE_BASE = 10000.0
LOG2E = 1.4426950408889634
INV_LN2 = LOG2E
SB_DEAD_LOG2 = 160.0
SB_HEADS = 8
SB_HEAD_DIM = 64
HG_HEADS = 4
HG_KEY = 128
N_BRANCH = 3

LANES = 128
MLA_IN_WIDTH = MLA_Q_LORA + MLA_KV_LORA + 2 * LANES
MXU_WIDTH = 256
ATT_BLOCK = MXU_WIDTH
FFN_CHUNK = MXU_WIDTH
PROJ_CHUNK = MXU_WIDTH
ROW_BLOCK = 512
CAST_ROWS = 256
HG_UNROLL = 32
VMEM_LIMIT_BYTES = 56 * 1024 * 1024

NT_DIMS = (((1,), (1,)), ((), ()))
TN_DIMS = (((0,), (0,)), ((), ()))


def _rms(x, w):
    return x * lax.rsqrt(jnp.mean(x * x, axis=-1, keepdims=True) + EPS) * w


def _dot(a, b):
    return jnp.dot(a, b, preferred_element_type=F32)


def _split_bf16(x, terms):
    parts = []
    for _ in range(terms - 1):
        hi = x.astype(BF16)
        parts.append(hi)
        x = x - hi.astype(F32)
    parts.append(x.astype(BF16))
    return parts


def _rot_half(w):
    half = MLA_ROPE // 2
    return jnp.concatenate([-w[..., half:], w[..., :half]], axis=-1)


def _params(n_axes):
    return pltpu.CompilerParams(dimension_semantics=("arbitrary",) * n_axes,
                                vmem_limit_bytes=VMEM_LIMIT_BYTES)


def _resident(shape):
    return pl.BlockSpec(shape, lambda *_: (0,) * len(shape), pipeline_mode=pl.Buffered(1))


def _cast_kernel(w_ref, o_ref):
    o_ref[...] = w_ref[0].astype(o_ref.dtype)


def _layer_bf16(w, layer):
    _, k, n = w.shape
    tk = CAST_ROWS if k % CAST_ROWS == 0 else k
    return pl.pallas_call(
        _cast_kernel,
        grid=(k // tk,),
        in_specs=[pl.BlockSpec((1, tk, n), lambda r: (layer, r, 0))],
        out_specs=pl.BlockSpec((tk, n), lambda r: (r, 0)),
        out_shape=jax.ShapeDtypeStruct((k, n), BF16),
        compiler_params=_params(1),
        name="to_bf16",
    )(w)


def _inproj_weight_kernel(w_ref, o_ref):
    w = w_ref[0]
    c_lat = MLA_Q_LORA + MLA_KV_LORA
    n_sb, n_hg = SB_HEADS * SB_HEAD_DIM, HG_HEADS * HG_KEY
    kr = w[:, c_lat:c_lat + MLA_ROPE]
    sb = w[:, c_lat + MLA_ROPE:c_lat + MLA_ROPE + 3 * n_sb]
    hg = w[:, c_lat + MLA_ROPE + 3 * n_sb:c_lat + MLA_ROPE + 3 * n_sb + 4 * n_hg]
    gates = w[:, c_lat + MLA_ROPE + 3 * n_sb + 4 * n_hg:]
    copies = LANES // MLA_ROPE
    parts = ([w[:, :c_lat]] + [kr] * copies + [_rot_half(kr)] * copies
             + [sb[:, :n_sb] * (SB_HEAD_DIM ** -0.5 * LOG2E), sb[:, n_sb:]]
             + [hg[:, n_hg:2 * n_hg], hg[:, :n_hg], hg[:, 2 * n_hg:]]
             + [gates])
    o_ref[...] = jnp.concatenate(parts, axis=1).astype(o_ref.dtype)


def _inproj_weight(w, layer, width):
    _, k, n = w.shape
    return pl.pallas_call(
        _inproj_weight_kernel,
        grid=(k // CAST_ROWS,),
        in_specs=[pl.BlockSpec((1, CAST_ROWS, n), lambda r: (layer, r, 0))],
        out_specs=pl.BlockSpec((CAST_ROWS, width), lambda r: (r, 0)),
        out_shape=jax.ShapeDtypeStruct((k, width), BF16),
        compiler_params=_params(1),
        name="inproj_weight",
    )(w)


def _ffn_kernel(*refs, d_ff, fc, mix, embed, final):
    refs = list(refs)
    h_ref = refs.pop(0)
    if mix:
        ya_ref, yb_ref, yc_ref, gate_ref, wa_ref, wb_ref, wc_ref, wo_ref = refs[:8]
        del refs[:8]
    nw_ref, win_ref, wout_ref = refs[:3]
    del refs[:3]
    if embed:
        p_ref, pn_ref, wg_ref, wp_ref, fw_ref = refs[:5]
        del refs[:5]
    o_ref, xn_ref, hid_ref = refs

    x = h_ref[...]
    if mix:
        d = x.shape[1]
        merged = None
        for j, (y_ref, w_ref) in enumerate(((ya_ref, wa_ref), (yb_ref, wb_ref), (yc_ref, wc_ref))):
            gate = jax.nn.sigmoid(gate_ref[:, j * d:(j + 1) * d].astype(F32))
            term = gate * _dot(y_ref[...], w_ref[...])
            merged = term if merged is None else merged + term
        x = x + _dot(merged.astype(BF16), wo_ref[...])
    xn_ref[...] = _rms(x, nw_ref[...]).astype(BF16)
    for c in range(d_ff // fc):
        xn = xn_ref[...]
        g = _dot(xn, win_ref[:, c * fc:(c + 1) * fc])
        up = _dot(xn, win_ref[:, d_ff + c * fc:d_ff + (c + 1) * fc])
        hid_ref[:, c * fc:(c + 1) * fc] = (g * jax.nn.sigmoid(g) * up).astype(BF16)
    h = x + 0.5 * _dot(hid_ref[...], wout_ref[...])
    if embed:
        gate = jax.nn.sigmoid(_dot(_rms(h, pn_ref[...]).astype(BF16), wg_ref[...]))
        h = h + _dot(p_ref[0].astype(BF16), wp_ref[...]) * gate
        if final:
            h = _rms(h, fw_ref[...])
    o_ref[...] = h


def _ffn(h, nw, w_in, w_out, mix=None, embed=None, final=False):
    t, d = h.shape
    d_ff = w_out.shape[0]
    tm = min(ROW_BLOCK, t)
    assert t % tm == 0 and d_ff % FFN_CHUNK == 0
    row = lambda n: pl.BlockSpec((tm, n), lambda i: (i, 0))
    in_specs, args = [row(d)], [h]
    if mix is not None:
        in_specs += [row(a.shape[1]) for a in mix[:4]] + [_resident(w.shape) for w in mix[4:]]
        args += list(mix)
    in_specs += [_resident((1, d)), _resident((d, 2 * d_ff)), _resident((d_ff, d))]
    args += [nw, w_in, w_out]
    if embed is not None:
        p, layer, pn, wg, wp, fw = embed
        in_specs += [pl.BlockSpec((1, tm, p.shape[2]), lambda i: (layer, i, 0)),
                     _resident(pn.shape), _resident(wg.shape), _resident(wp.shape), _resident(fw.shape)]
        args += [p, pn, wg, wp, fw]
    return pl.pallas_call(
        functools.partial(_ffn_kernel, d_ff=d_ff, fc=FFN_CHUNK, mix=mix is not None, embed=embed is not None,
                          final=final),
        grid=(t // tm,),
        in_specs=in_specs,
        out_specs=row(d),
        out_shape=jax.ShapeDtypeStruct((t, d), F32),
        scratch_shapes=[pltpu.VMEM((tm, d), BF16), pltpu.VMEM((tm, d_ff), BF16)],
        compiler_params=_params(1),
        name="ffn",
    )(*args)


def _inproj_kernel(h_ref, nw_ref, w_ref, *o_refs, widths, nc):
    u = _rms(h_ref[...], nw_ref[...]).astype(BF16)
    start = 0
    for o_ref, width in zip(o_refs, widths):
        for c in range(0, width, nc):
            n = min(nc, width - c)
            o_ref[:, c:c + n] = _dot(u, w_ref[:, start + c:start + c + n]).astype(o_ref.dtype)
        start += width


def _inproj(h, nw, w, widths, dtypes):
    t, d = h.shape
    tm = min(ROW_BLOCK, t)
    assert t % tm == 0 and sum(widths) == w.shape[1] and all(n % LANES == 0 for n in widths)
    return pl.pallas_call(
        functools.partial(_inproj_kernel, widths=widths, nc=PROJ_CHUNK),
        grid=(t // tm,),
        in_specs=[pl.BlockSpec((tm, d), lambda i: (i, 0)), _resident((1, d)), _resident(w.shape)],
        out_specs=[pl.BlockSpec((tm, n), lambda i: (i, 0)) for n in widths],
        out_shape=[jax.ShapeDtypeStruct((t, n), dt) for n, dt in zip(widths, dtypes)],
        compiler_params=_params(1),
        name="inproj",
    )(h, nw, w)


def _rope_kernel(pos_ref, inv_ref, cos_ref, sin_ref):
    ang = pos_ref[0].astype(F32) * inv_ref[...]
    cos_ref[0] = jnp.cos(ang)
    sin_ref[0] = jnp.sin(ang)


def _rope_tables(pos, inv):
    b, seq, _ = pos.shape
    out = jax.ShapeDtypeStruct((b, seq, LANES), F32)
    spec = pl.BlockSpec((1, seq, LANES), lambda i: (i, 0, 0))
    return pl.pallas_call(
        _rope_kernel,
        grid=(b,),
        in_specs=[pl.BlockSpec((1, seq, 1), lambda i: (i, 0, 0)), _resident(inv.shape)],
        out_specs=[spec, spec],
        out_shape=[out, out],
        compiler_params=_params(1),
        name="rope_tables",
    )(pos, inv)


def _mla_kernel(x_ref, cos_ref, sin_ref, qn_ref, kvn_ref, wqn_ref, wqr_ref, wqrp_ref, wk_ref, wv_ref,
                o_ref, qcat_ref, kcat_ref, v_ref, *, seq, tq, scale):
    p = pl.program_id(1)
    n_pairs = MLA_HEADS // 2

    @pl.when(p == 0)
    def _project():
        rb = min(ROW_BLOCK, seq)
        for r in range(seq // rb):
            rows = slice(r * rb, (r + 1) * rb)
            x = x_ref[0, rows, :]
            c_q = x[:, :MLA_Q_LORA].astype(F32)
            c_kv = x[:, MLA_Q_LORA:MLA_Q_LORA + MLA_KV_LORA].astype(F32)
            kr = x[:, MLA_Q_LORA + MLA_KV_LORA:MLA_Q_LORA + MLA_KV_LORA + LANES].astype(F32)
            krp = x[:, MLA_Q_LORA + MLA_KV_LORA + LANES:].astype(F32)
            cos, sin = cos_ref[0, rows, :], sin_ref[0, rows, :]
            cqn = _rms(c_q, qn_ref[...]).astype(BF16)
            ckvn = _rms(c_kv, kvn_ref[...]).astype(BF16)
            k_rope = (kr * cos + krp * sin).astype(BF16)
            halves = lambda a: (a[:, :LANES], a[:, LANES:])
            ropes = zip(halves(_dot(cqn, wqr_ref[...])), halves(_dot(cqn, wqrp_ref[...])))
            for g, (qr, qrp) in enumerate(ropes):
                q_rope = ((qr * cos + qrp * sin) * scale).astype(BF16)
                qcat_ref[2 * g, rows, LANES:] = q_rope
                qcat_ref[2 * g + 1, rows, LANES:] = q_rope
            for g in range(n_pairs // 2):
                cols = slice(2 * g * LANES, (2 * g + 2) * LANES)
                parts = zip(halves(_dot(cqn, wqn_ref[:, cols])), halves(_dot(ckvn, wk_ref[:, cols])),
                            halves(_dot(ckvn, wv_ref[:, cols])))
                for pp, (qn, kn, vv) in enumerate(parts, start=2 * g):
                    qcat_ref[pp, rows, :LANES] = (qn * scale).astype(BF16)
                    kcat_ref[pp, rows, :LANES] = kn.astype(BF16)
                    kcat_ref[pp, rows, LANES:] = k_rope
                    v_ref[pp, rows, :] = vv.astype(BF16)

    lane = lax.broadcasted_iota(jnp.int32, (1, 2 * LANES), 1)
    out_lane = lax.broadcasted_iota(jnp.int32, (1, LANES), 1)
    ri = lax.broadcasted_iota(jnp.int32, (tq, tq), 0)
    ci = lax.broadcasted_iota(jnp.int32, (tq, tq), 1)
    diag_mask = (ci // CHUNK) <= (ri // CHUNK)
    group = jnp.where(lane < LANES, lane // MLA_NOPE, 2 + (lane - LANES) // MLA_ROPE)

    for n in range(seq // tq):
        keys = [slice(j * tq, (j + 1) * tq) for j in range(n + 1)]
        qc = qcat_ref[p, keys[n], :]
        s = []
        for hh in range(2):
            own = (group == hh) | (group == 2 + (p % 2) * 2 + hh)
            qh = qc * jnp.where(own, 1.0, 0.0).astype(BF16)
            sh = [lax.dot_general(qh, kcat_ref[p, kj, :], NT_DIMS, preferred_element_type=F32)
                  for kj in keys]
            sh[n] = jnp.where(diag_mask, sh[n], -jnp.inf)
            s.append(sh)
        m = [jnp.max(functools.reduce(jnp.maximum, sh), axis=-1, keepdims=True) for sh in s]
        outs = []
        for hh in range(2):
            pr = [jnp.exp2(sj - m[hh]) for sj in s[hh]]
            l = jnp.sum(functools.reduce(jnp.add, pr), axis=-1, keepdims=True)
            acc = None
            for j, kj in enumerate(keys):
                t = _dot(pr[j].astype(BF16), v_ref[p, kj, :])
                acc = t if acc is None else acc + t
            outs.append(acc * (1.0 / l))
        o_ref[0, keys[n], :] = jnp.where(out_lane < MLA_V, outs[0], outs[1]).astype(o_ref.dtype)


def _sb_kernel(q_ref, k_ref, v_ref, tri_ref, o_ref, *, tq, n_q):
    lane = lax.broadcasted_iota(jnp.int32, (1, LANES), 1)
    ri = lax.broadcasted_iota(jnp.int32, (tq, tq), 0)
    ci = lax.broadcasted_iota(jnp.int32, (tq, tq), 1)
    diag_mask = ci < ri
    sign = jnp.int32(-2 ** 31)
    rows = [slice(j * tq, (j + 1) * tq) for j in range(n_q)]

    def sweep(blocks, state):
        tiles = [(n, hh, j) for n, js in blocks.items() for j in js for hh in range(2)]
        qhs = {}
        for n in blocks:
            q = q_ref[0, rows[n], :]
            for hh in range(2):
                qhs[n, hh] = jnp.where((lane // SB_HEAD_DIM) == hh, q, jnp.zeros_like(q))
        z2 = {t: lax.dot_general(qhs[t[0], t[1]], k_ref[0, rows[t[2]], :], NT_DIMS,
                                 preferred_element_type=F32) for t in tiles}
        suffix, total = {}, {}
        for t in tiles:
            neg_abs = lax.bitcast_convert_type(lax.bitcast_convert_type(z2[t], jnp.int32) | sign, F32)
            sp2 = jnp.maximum(z2[t], 0.0) + jnp.log(1.0 + jnp.exp2(neg_abs)) * INV_LN2
            keep = jnp.where(diag_mask, sp2, 0.0) if t[2] == t[0] else sp2
            suffix[t] = _dot(keep.astype(BF16), tri_ref[...])
            total[t] = jnp.sum(keep, axis=-1, keepdims=True)
        state = dict(state)
        for n, hh, j in tiles:
            rest, acc = state[n, hh]
            again = lax.dot_general(qhs[n, hh] * jnp.ones((1, LANES), BF16), k_ref[0, rows[j], :], NT_DIMS,
                                    preferred_element_type=F32)
            a = jnp.exp2(jnp.minimum(again - suffix[n, hh, j] - rest, 0.0))
            if j == n:
                a = jnp.where(diag_mask, a, 0.0)
            pv = _dot(a.astype(BF16), v_ref[0, rows[j], :])
            state[n, hh] = (rest + total[n, hh, j], pv if acc is None else acc + pv)
        return state

    def store(n, state):
        o_ref[0, rows[n], :] = jnp.where(lane < SB_HEAD_DIM, state[n, 0][1], state[n, 1][1]).astype(o_ref.dtype)

    start = {(n, hh): (jnp.zeros((tq, 1), F32), None) for n in range(n_q) for hh in range(2)}
    near = sweep({n: [j for j in (n, n - 1) if j >= 0] for n in range(n_q)}, start)
    for n in range(n_q):
        store(n, near)

    for n in range(2, n_q):
        alive = jnp.minimum(jnp.min(near[n, 0][0]), jnp.min(near[n, 1][0])) < SB_DEAD_LOG2

        @pl.when(alive)
        def _(n=n):
            far = sweep({n: list(range(n - 2, -1, -1))}, {k: v for k, v in near.items() if k[0] == n})
            store(n, far)


def _mla_sb_kernel(x_ref, cos_ref, sin_ref, qn_ref, kvn_ref, wqn_ref, wqr_ref, wqrp_ref, wk_ref, wv_ref,
                   sq_ref, sk_ref, sv_ref, tri_ref, oa_ref, ob_ref, qcat_ref, kcat_ref, v_ref, *, seq, tq, scale):
    _mla_kernel(x_ref, cos_ref, sin_ref, qn_ref, kvn_ref, wqn_ref, wqr_ref, wqrp_ref, wk_ref, wv_ref,
                oa_ref, qcat_ref, kcat_ref, v_ref, seq=seq, tq=tq, scale=scale)
    _sb_kernel(sq_ref, sk_ref, sv_ref, tri_ref, ob_ref, tq=tq, n_q=seq // tq)


def _mla_sb(x, cos, sin, qn, kvn, wqn, wqr, wqrp, wk, wv, qkv, tri):
    b, seq, _ = x.shape
    assert MLA_HEADS == SB_HEADS and MLA_V == SB_HEAD_DIM
    n_pairs = MLA_HEADS // 2
    tq = min(ATT_BLOCK, seq)
    assert seq % tq == 0 and seq % min(ROW_BLOCK, seq) == 0 and tq % CHUNK == 0
    scale = float((MLA_NOPE + MLA_ROPE) ** -0.5) * LOG2E
    col = lambda off: pl.BlockSpec((1, seq, LANES), lambda i, j: (i, 0, off + j))
    full = lambda n: pl.BlockSpec((1, seq, n), lambda i, j: (i, 0, 0))
    out = jax.ShapeDtypeStruct((b, seq, MLA_HEADS * MLA_V), BF16)
    return pl.pallas_call(
        functools.partial(_mla_sb_kernel, seq=seq, tq=tq, scale=scale),
        grid=(b, n_pairs),
        in_specs=[full(MLA_IN_WIDTH), full(LANES), full(LANES),
                  _resident(qn.shape), _resident(kvn.shape),
                  _resident(wqn.shape), _resident(wqr.shape), _resident(wqrp.shape),
                  _resident(wk.shape), _resident(wv.shape),
                  col(0), col(n_pairs), col(2 * n_pairs), _resident(tri.shape)],
        out_specs=[col(0), col(0)],
        out_shape=[out, out],
        scratch_shapes=[pltpu.VMEM((n_pairs, seq, 2 * LANES), BF16),
                        pltpu.VMEM((n_pairs, seq, 2 * LANES), BF16),
                        pltpu.VMEM((n_pairs, seq, LANES), BF16)],
        compiler_params=_params(2),
        name="mla_sb",
    )(x, cos, sin, qn, kvn, wqn, wqr, wqrp, wk, wv, qkv, qkv, qkv, tri)


def _hgrn_kernel(f_ref, qig_ref, lbp_ref, nw_ref, tri_ref, o_ref, b_ref, *, seq, layer, unroll):
    width = HG_HEADS * HG_KEY
    lbp = lbp_ref[...]
    e = jnp.exp(lbp - jnp.max(lbp, axis=0, keepdims=True))
    sm = e / jnp.sum(e, axis=0, keepdims=True)
    lb = jnp.zeros((1, width), F32)
    for j in range(1, layer + 1):
        lb = lb + sm[j:j + 1, :]
    lb = jnp.clip(lb, 0.0, 1.0 - 1e-6)
    lb_floor = jnp.maximum(lb, LB_FLOOR)

    gb = tri_ref.shape[0]
    for r in range(seq // gb):
        rows = slice(r * gb, (r + 1) * gb)
        log_f = jnp.minimum(jnp.log(lb_floor + (1.0 - lb) * jax.nn.sigmoid(f_ref[0, rows, :])), 0.0)
        acc = None
        for part in _split_bf16(log_f * LOG2E, 2):
            t = _dot(tri_ref[...], part)
            acc = t if acc is None else acc + t
        b_ref[rows, :] = acc

    n_sub = CHUNK // HG_SUB
    causal = (lax.broadcasted_iota(jnp.int32, (CHUNK, CHUNK), 1)
              <= lax.broadcasted_iota(jnp.int32, (CHUNK, CHUNK), 0))

    def step(n, states):
        states = list(states)
        tiles = [(u, h) for u in range(unroll) for h in range(HG_HEADS)]
        rows = {u: pl.ds(pl.multiple_of((n * unroll + u) * CHUNK, CHUNK), CHUNK) for u in range(unroll)}
        cols = {h: slice(h * HG_KEY, (h + 1) * HG_KEY) for h in range(HG_HEADS)}

        work = {}
        for u, h in tiles:
            x = f_ref[0, rows[u], cols[h]]
            b = b_ref[rows[u], cols[h]]
            q_raw = qig_ref[0, rows[u], cols[h]].astype(F32)
            q = q_raw * jax.nn.sigmoid(q_raw)
            k = (1.0 - lb[:, cols[h]]) * jax.nn.sigmoid(-x)
            v16 = qig_ref[0, rows[u], width + h * HG_KEY:width + (h + 1) * HG_KEY]
            b_last = b[CHUNK - 1:CHUNK, :]
            qd = (q * jnp.exp2(b)).astype(BF16)
            kd = (k * jnp.exp2(b_last - b)).astype(BF16)
            update = lax.dot_general(v16, kd, TN_DIMS, preferred_element_type=F32)
            qf, kf = [], []
            for i in range(n_sub):
                lo, hi = i * HG_SUB, (i + 1) * HG_SUB
                b_ref_i = jnp.zeros((1, HG_KEY), F32) if i == 0 else b[lo - 1:lo, :]
                qf_i = q[lo:hi, :] * jnp.exp2(b[lo:hi, :] - b_ref_i)
                above = [jnp.zeros((lo, HG_KEY), F32)] if lo else []
                below = [jnp.zeros((CHUNK - hi, HG_KEY), F32)] if hi < CHUNK else []
                qf.append(jnp.concatenate(above + [qf_i] + below, axis=0))
                kf.append(jnp.concatenate([k[:hi, :] * jnp.exp2(b_ref_i - b[:hi, :])] + below, axis=0))
            a = lax.dot_general(jnp.concatenate(qf, axis=1).astype(BF16),
                                jnp.concatenate(kf, axis=1).astype(BF16), NT_DIMS,
                                preferred_element_type=F32)
            a = jnp.where(causal, a, 0.0).astype(BF16)
            work[u, h] = (qd, update, jnp.exp2(b_last), a, v16)

        inter = {}
        for u, h in tiles:
            qd, update, decay, _, _ = work[u, h]
            inter[u, h] = lax.dot_general(qd, states[h].astype(BF16), NT_DIMS, preferred_element_type=F32)
            states[h] = states[h] * decay + update

        for u, h in tiles:
            _, _, _, a, v16 = work[u, h]
            o = inter[u, h] + _dot(a, v16)
            o = o * lax.rsqrt(jnp.mean(o * o, axis=-1, keepdims=True) + EPS) * nw_ref[:, cols[h]]
            g = qig_ref[0, rows[u], 2 * width + h * HG_KEY:2 * width + (h + 1) * HG_KEY].astype(F32)
            o_ref[0, rows[u], cols[h]] = (o * (g * jax.nn.sigmoid(g))).astype(o_ref.dtype)
        return tuple(states)

    zero = jnp.zeros((HG_KEY, HG_KEY), F32)
    lax.fori_loop(0, seq // (CHUNK * unroll), step, (zero,) * HG_HEADS)


def _hgrn(hg_f, hg_qig, lbp, nw, tri, layer):
    b, seq, width = hg_f.shape
    unroll = min(HG_UNROLL, seq // CHUNK)
    assert seq % (CHUNK * unroll) == 0 and seq % tri.shape[0] == 0
    return pl.pallas_call(
        functools.partial(_hgrn_kernel, seq=seq, layer=layer, unroll=unroll),
        grid=(b,),
        in_specs=[pl.BlockSpec((1, seq, width), lambda i: (i, 0, 0)),
                  pl.BlockSpec((1, seq, 3 * width), lambda i: (i, 0, 0)),
                  _resident(lbp.shape), _resident(nw.shape), _resident(tri.shape)],
        out_specs=pl.BlockSpec((1, seq, width), lambda i: (i, 0, 0)),
        out_shape=jax.ShapeDtypeStruct((b, seq, width), BF16),
        scratch_shapes=[pltpu.VMEM((seq, width), F32)],
        compiler_params=_params(1),
        name="hgrn2",
    )(hg_f, hg_qig, lbp, nw, tri)


def _lower(n):
    r = lax.broadcasted_iota(jnp.int32, (n, n), 0)
    c = lax.broadcasted_iota(jnp.int32, (n, n), 1)
    return r >= c


def kernel(x, p, positions, ffn_a_norm, ffn_a_w_in, ffn_a_w_out, mix_norm, w_in, mla_q_norm, mla_w_uq, mla_kv_norm, mla_w_ukv, hgrn_lower_bounds, hgrn_out_norm, w_br_mla, w_br_sb, w_br_hgrn, w_out, ffn_b_norm, ffn_b_w_in, ffn_b_w_out, ple_norm, w_ple_gate, w_ple_proj, final_norm):
    b, seq, d = x.shape
    depth = ffn_a_norm.shape[0]
    t = b * seq
    bf = lambda a: a.astype(BF16)
    row = lambda a: a.reshape(1, -1).astype(F32)

    tq = min(ATT_BLOCK, seq)
    sb_tri = _lower(tq).astype(BF16)
    gb = 4 * CHUNK
    r = lax.broadcasted_iota(jnp.int32, (gb, gb), 0)
    c = lax.broadcasted_iota(jnp.int32, (gb, gb), 1)
    hg_tri = ((c <= r) & (c // CHUNK == r // CHUNK)).astype(BF16)

    half = MLA_ROPE // 2
    inv = ROPE_BASE ** (-jnp.arange(half, dtype=F32) / half)
    inv = jnp.tile(inv, LANES // half).reshape(1, LANES)
    cos, sin = _rope_tables(positions.reshape(b, seq, 1), inv)

    hg_w = HG_HEADS * HG_KEY
    widths = (MLA_IN_WIDTH, 3 * SB_HEADS * SB_HEAD_DIM, hg_w, 3 * hg_w, N_BRANCH * d)

    h = x.reshape(t, d)
    for i in range(depth):
        h = _ffn(h, row(ffn_a_norm[i]), _layer_bf16(ffn_a_w_in, i), _layer_bf16(ffn_a_w_out, i))

        w_cat = _inproj_weight(w_in, i, sum(widths))
        mla_in, sb_in, hg_f, hg_qig, gates = _inproj(h, row(mix_norm[i]), w_cat, widths,
                                                     (BF16, BF16, F32, BF16, BF16))

        wq = mla_w_uq[i].reshape(MLA_Q_LORA, MLA_HEADS, MLA_NOPE + MLA_ROPE)
        wq_nope = bf(wq[:, :, :MLA_NOPE].reshape(MLA_Q_LORA, -1))
        wq_rope = wq[:, :, MLA_NOPE:]
        wkv = mla_w_ukv[i].reshape(MLA_KV_LORA, MLA_HEADS, MLA_NOPE + MLA_V)
        y_a, y_b = _mla_sb(
            mla_in.reshape(b, seq, -1), cos, sin, row(mla_q_norm[i]), row(mla_kv_norm[i]),
            wq_nope, bf(wq_rope.reshape(MLA_Q_LORA, -1)), bf(_rot_half(wq_rope).reshape(MLA_Q_LORA, -1)),
            bf(wkv[:, :, :MLA_NOPE].reshape(MLA_KV_LORA, -1)), bf(wkv[:, :, MLA_NOPE:].reshape(MLA_KV_LORA, -1)),
            sb_in.reshape(b, seq, -1), sb_tri)
        y_c = _hgrn(hg_f.reshape(b, seq, -1), hg_qig.reshape(b, seq, -1), hgrn_lower_bounds.astype(F32),
                    row(hgrn_out_norm[i]), hg_tri, i)

        h = _ffn(h, row(ffn_b_norm[i]), _layer_bf16(ffn_b_w_in, i), _layer_bf16(ffn_b_w_out, i),
                 mix=(y_a.reshape(t, -1), y_b.reshape(t, -1), y_c.reshape(t, -1), gates, _layer_bf16(w_br_mla, i),
                      _layer_bf16(w_br_sb, i), _layer_bf16(w_br_hgrn, i), _layer_bf16(w_out, i)),
                 embed=(p.reshape(depth, t, -1), i, row(ple_norm[i]), _layer_bf16(w_ple_gate, i),
                        _layer_bf16(w_ple_proj, i), row(final_norm)),
                 final=(i == depth - 1))
    return h.reshape(b, seq, d)
```
